```python
import math, functools
import jax, jax.numpy as jnp
from jax import lax
import numpy as np

D_MODEL = 2048
BATCH = 1
SEQ = 8192
DEPTH = 1
DEC_BATCH = 32
DEC_SEQ = 8
PAST_LEN = 16384
PAGE_SIZE = 128

D_RNN = D_MODEL
RNN_BLOCKS = 8
RNN_BLK = D_RNN // RNN_BLOCKS
CONV_W = 4
C_SCALE = 8.0
N_HEADS = 16
HEAD_DIM = 128
N_KV = 4
GRP = N_HEADS // N_KV
KV_W = N_KV * HEAD_DIM
L_CMP = 32
L_SLC = 64
CMP_PER_SLC = L_SLC // L_CMP
SLC_PER_PAGE = PAGE_SIZE // L_SLC
N_SEL = 16
N_LOCAL = 2
WINDOW = 512
CMP_HIDDEN = HEAD_DIM
Q_BLK = 64
WIN_BLK = 128
FORCE = 1e4
SCALE = HEAD_DIM ** -0.5
ROPE_THETA = 10000.0
PLE_DIM = 256
EPS = 1e-6
SIZES = (D_RNN, D_RNN, N_HEADS * HEAD_DIM, 6 * KV_W, N_HEADS * HEAD_DIM, 3 * N_HEADS, 2 * D_MODEL)
D_IN = sum(SIZES)
SPLIT_AT = tuple(sum(SIZES[:i + 1]) for i in range(len(SIZES) - 1))

kernel_name = 'hawk_nsa_gated_hybrid_step'

f32 = jnp.float32


def _rmsnorm(x, g):
    xf = x.astype(f32)
    y = xf * lax.rsqrt(jnp.mean(xf * xf, axis=-1, keepdims=True) + EPS)
    return (y * g.astype(f32)).astype(x.dtype)


def _rope(x, pos):
    half = HEAD_DIM // 2
    inv = ROPE_THETA ** (-jnp.arange(half, dtype=f32) / half)
    ang = pos.astype(f32)[:, None] * inv[None, :]
    shp = (pos.shape[0],) + (1,) * (x.ndim - 3) + (half,)
    cos = jnp.cos(ang).reshape(shp)
    sin = jnp.sin(ang).reshape(shp)
    xf = x.astype(f32)
    x1, x2 = xf[..., :half], xf[..., half:]
    return jnp.concatenate([x1 * cos - x2 * sin, x2 * cos + x1 * sin], axis=-1).astype(x.dtype)


def _masked_softmax(s, mask):
    s = jnp.where(mask, s, -jnp.inf)
    m = jnp.max(s, axis=-1, keepdims=True)
    m = jnp.where(jnp.isfinite(m), m, 0.0)
    e = jnp.where(mask, jnp.exp(s - m), 0.0)
    return e / jnp.maximum(jnp.sum(e, axis=-1, keepdims=True), 1e-30)


def _attend(q, k, v, mask):
    s = jnp.einsum('...qgrd,...kgd->...qgrk', q.astype(f32), k.astype(f32)) * SCALE
    p = _masked_softmax(s, mask)
    return jnp.einsum('...qgrk,...kgd->...qgrd', p, v.astype(f32)), p


def _lin_comb(left, right):
    a1, b1 = left
    a2, b2 = right
    return a1 * a2, a2 * b1 + b2


def _conv_rglru(xr, conv_state, h0, w_conv, b_conv, w_rg, b_rg, lam):
    B, T, _ = xr.shape
    xp = jnp.concatenate([conv_state.astype(xr.dtype), xr], axis=1)
    xpf = xp.astype(f32)
    xc = b_conv.astype(f32)
    for k in range(CONV_W):
        xc = xc + xpf[:, k:k + T] * w_conv[k].astype(f32)
    g = jnp.einsum('btnc,encd->ebtnd', xc.reshape(B, T, RNN_BLOCKS, RNN_BLK), w_rg.astype(f32))
    g = g.reshape(2, B, T, D_RNN) + b_rg.astype(f32)[:, None, None, :]
    r = jax.nn.sigmoid(g[0])
    i = jax.nn.sigmoid(g[1])
    log_a = -C_SCALE * r * jax.nn.softplus(-lam.astype(f32))
    a = jnp.exp(log_a)
    b = jnp.sqrt(-jnp.expm1(2.0 * log_a)) * (i * xc)
    b = b.at[:, 0].add(a[:, 0] * h0.astype(f32))
    _, h = lax.associative_scan(_lin_comb, (a, b), axis=1)
    return h, xp[:, -(CONV_W - 1):], h[:, -1].astype(xr.dtype)


def _compress(kv, w1, w2, pe):
    B = kv.shape[0]
    nc = kv.shape[1] // L_CMP
    blk = kv.reshape(B, nc, L_CMP, 2, N_KV, HEAD_DIM).astype(f32) + jnp.transpose(pe, (1, 0, 2))[:, :, None, :].astype(f32)
    hid = jax.nn.silu(jnp.einsum('bclegd,eldh->bcegh', blk, w1.astype(f32)))
    return jnp.einsum('bcegh,ehd->bcegd', hid, w2.astype(f32))


def _cmp_mask(pos, nc):
    return ((jnp.arange(nc) + 1) * L_CMP - 1)[None, :] <= pos[:, None]


def _select_blocks(p, pos, n_blk):
    B, T, G, R, nc = p.shape
    ps = p.sum(axis=3)
    ps = jnp.pad(ps, ((0, 0), (0, 0), (0, 0), (0, CMP_PER_SLC * n_blk - nc)))
    ps = ps.reshape(B, T, G, n_blk, CMP_PER_SLC).sum(-1)
    jb = jnp.arange(n_blk)[None, :]
    jt = (pos // L_SLC)[:, None]
    forced = (jb == 0) | ((jb >= jt - (N_LOCAL - 1)) & (jb <= jt))
    future = jb > jt
    score = jnp.where(future[None, :, None, :], -jnp.inf, jnp.where(forced[None, :, None, :], FORCE, ps))
    _, idx = lax.top_k(score, min(N_SEL, n_blk))
    return idx


def _slc_core(q, kv, idx, t_pos):
    Tq, G, N = idx.shape
    kpos = idx[..., None] * L_SLC + jnp.arange(L_SLC)
    mask = (kpos <= t_pos[:, None, None, None]).reshape(Tq, G, 1, N * L_SLC)
    k = kv[..., 0, :].reshape(Tq, G, N * L_SLC, HEAD_DIM).astype(f32)
    v = kv[..., 1, :].reshape(Tq, G, N * L_SLC, HEAD_DIM).astype(f32)
    s = jnp.einsum('tgrd,tgmd->tgrm', q.astype(f32), k) * SCALE
    p = _masked_softmax(s, mask)
    return jnp.einsum('tgrm,tgmd->tgrd', p, v)


def _slc_seq(q, idx, pos, gather):
    T = q.shape[0]
    qb = math.gcd(T, Q_BLK)
    nb = T // qb
    k = idx.shape[-1]

    def body(args):
        qq, ii, tt = args
        return _slc_core(qq, gather(ii), ii, tt)

    o = lax.map(body, (q.reshape(nb, qb, N_KV, GRP, HEAD_DIM), idx.reshape(nb, qb, N_KV, k), pos.reshape(nb, qb)))
    return o.reshape(T, N_KV, GRP, HEAD_DIM)


def _gather_blocks(kb, ii):
    return kb[ii, :, :, jnp.arange(N_KV)[None, :, None]]


def _gather_paged(pool, l, pt, newb, nb_past, ii):
    phys = pt[jnp.minimum(ii // SLC_PER_PAGE, pt.shape[0] - 1)]
    rows = (ii % SLC_PER_PAGE)[..., None] * L_SLC + jnp.arange(L_SLC)
    from_pool = pool[l, phys[..., None], rows, :, jnp.arange(N_KV)[None, :, None, None]]
    from_new = _gather_blocks(newb, jnp.clip(ii - nb_past, 0, newb.shape[0] - 1))
    return jnp.where((ii < nb_past)[..., None, None, None], from_pool.astype(from_new.dtype), from_new)


def _win_banded(q, kv):
    B, T = q.shape[:2]
    wb = math.gcd(T, WIN_BLK)
    nq = T // wb
    K = WINDOW + wb
    kvp = jnp.pad(kv, ((0, 0), (WINDOW, 0), (0, 0), (0, 0), (0, 0)))
    ridx = jnp.arange(nq)[:, None] * wb + jnp.arange(K)[None, :]
    kvg = kvp[:, ridx]
    r = jnp.arange(K)
    d = jnp.arange(wb)[:, None] + WINDOW - r[None, :]
    kpos = jnp.arange(nq)[:, None, None] * wb - WINDOW + r[None, None, :]
    mask = ((d >= 0) & (d <= WINDOW))[None] & (kpos >= 0)
    o, _ = _attend(q.reshape(B, nq, wb, N_KV, GRP, HEAD_DIM), kvg[:, :, :, 0], kvg[:, :, :, 1], mask[:, :, None, None, :])
    return o.reshape(B, T, N_KV, GRP, HEAD_DIM)


def _nsa_prompt(q, kv_cmp, kv_slc, kv_win, w_cmp1, w_cmp2, pe_cmp):
    B, T = q.shape[:2]
    pos = jnp.arange(T)
    nc = T // L_CMP
    ckv = _compress(kv_cmp[:, :nc * L_CMP], w_cmp1, w_cmp2, pe_cmp)
    o_cmp, p_cmp = _attend(q, ckv[:, :, 0], ckv[:, :, 1], _cmp_mask(pos, nc)[:, None, None, :])
    n_blk = -(-T // L_SLC)
    idx = _select_blocks(p_cmp, pos, n_blk)
    kvb = jnp.pad(kv_slc, ((0, 0), (0, n_blk * L_SLC - T), (0, 0), (0, 0), (0, 0)))
    kvb = kvb.reshape(B, n_blk, L_SLC, 2, N_KV, HEAD_DIM)

    def per_seq(args):
        qb, ib, kb = args
        return _slc_seq(qb, ib, pos, lambda ii: _gather_blocks(kb, ii))

    o_slc = lax.map(per_seq, (q, idx, kvb))
    o_win = _win_banded(q, kv_win)
    return o_cmp, o_slc, o_win, kv_win[:, T - min(WINDOW, T):]


def _nsa_sample(cache_cmp, cache_slc, l, page_table, win_buf, q, kv_cmp, kv_slc, kv_win, w_cmp1, w_cmp2, pe_cmp):
    B, T = q.shape[:2]
    past = page_table.shape[1] * PAGE_SIZE
    pos = past + jnp.arange(T)

    def cmp_past(pt):
        rows = cache_cmp[l, pt].reshape(1, past, 2, N_KV, HEAD_DIM)
        return _compress(rows, w_cmp1, w_cmp2, pe_cmp)[0]

    nc_new = T // L_CMP
    ckv = jnp.concatenate([lax.map(cmp_past, page_table),
                           _compress(kv_cmp[:, :nc_new * L_CMP], w_cmp1, w_cmp2, pe_cmp)], axis=1)
    nc = ckv.shape[1]
    o_cmp, p_cmp = _attend(q, ckv[:, :, 0], ckv[:, :, 1], _cmp_mask(pos, nc)[:, None, None, :])
    n_blk = -(-(past + T) // L_SLC)
    nb_past = past // L_SLC
    nbn = n_blk - nb_past
    idx = _select_blocks(p_cmp, pos, n_blk)
    newb = jnp.pad(kv_slc, ((0, 0), (0, nbn * L_SLC - T), (0, 0), (0, 0), (0, 0)))
    newb = newb.reshape(B, nbn, L_SLC, 2, N_KV, HEAD_DIM)

    def per_seq(args):
        qb, ib, pt, nbk = args
        return _slc_seq(qb, ib, pos, lambda ii: _gather_paged(cache_slc, l, pt, nbk, nb_past, ii))

    o_slc = lax.map(per_seq, (q, idx, page_table, newb))
    wb = win_buf.shape[1]
    kvw = jnp.concatenate([win_buf.astype(kv_win.dtype), kv_win], axis=1)
    kpos = past - wb + jnp.arange(wb + T)
    d = pos[:, None] - kpos[None, :]
    o_win, _ = _attend(q, kvw[:, :, 0], kvw[:, :, 1], ((d >= 0) & (d <= WINDOW))[:, None, None, :])
    keep = min(WINDOW, past + T)
    return o_cmp, o_slc, o_win, kvw[:, wb + T - keep:]


def _layer(x, p_l, pos, conv_state, h0, nsa_fn, g_norm, w_in, w_conv, b_conv, w_rg, b_rg, lam,
           w_cmp1, w_cmp2, pe_cmp, w_rnn_proj, w_nsa_proj, w_out, w_ple, w_ple_gate):
    B, T, _ = x.shape
    dt = x.dtype
    hn = _rmsnorm(x, g_norm)
    z = hn @ w_in
    xr, gr, q, kv, gn, bg, mg = jnp.split(z, SPLIT_AT, axis=-1)
    hr, conv_new, h_last = _conv_rglru(xr, conv_state, h0, w_conv, b_conv, w_rg, b_rg, lam)
    y_rnn = (hr.astype(dt) * jax.nn.silu(gr)) @ w_rnn_proj
    q = _rope(q.reshape(B, T, N_HEADS, HEAD_DIM), pos).reshape(B, T, N_KV, GRP, HEAD_DIM)
    kv = kv.reshape(B, T, 3, 2, N_KV, HEAD_DIM)
    kv = jnp.stack([_rope(kv[:, :, :, 0], pos), kv[:, :, :, 1]], axis=3)
    kv_cmp, kv_slc, kv_win = kv[:, :, 0], kv[:, :, 1], kv[:, :, 2]
    o_cmp, o_slc, o_win, win_new = nsa_fn(q, kv_cmp, kv_slc, kv_win, w_cmp1, w_cmp2, pe_cmp)
    bgs = jax.nn.sigmoid(bg.astype(f32)).reshape(B, T, 3, N_KV, GRP, 1)
    o = bgs[:, :, 0] * o_cmp + bgs[:, :, 1] * o_slc + bgs[:, :, 2] * o_win
    y_nsa = (o.reshape(B, T, N_HEADS * HEAD_DIM).astype(dt) * jax.nn.silu(gn)) @ w_nsa_proj
    mgs = jax.nn.sigmoid(mg).reshape(B, T, 2, D_MODEL)
    x = x + (mgs[:, :, 0] * y_rnn + mgs[:, :, 1] * y_nsa) @ w_out
    x = x + (p_l @ w_ple) * jax.nn.sigmoid(x @ w_ple_gate)
    return x, kv_cmp, kv_slc, win_new, conv_new, h_last


def setup_inputs(seed: int = 0) -> dict:
    key = jax.random.key(seed)
    ks = jax.random.split(key, 32)
    n_pages = PAST_LEN // PAGE_SIZE
    n_used = DEC_BATCH * n_pages
    n_pool = n_used + max(1, n_used // 4)
    nrm = jax.random.normal
    page_table = jax.random.permutation(ks[0], n_pool)[:n_used].reshape(DEC_BATCH, n_pages).astype(jnp.int32)
    a0 = jax.random.uniform(ks[1], (DEPTH, D_RNN), minval=0.9, maxval=0.999)
    s0 = a0 ** (1.0 / C_SCALE)
    lam = jnp.log(s0) - jnp.log1p(-s0)
    return {
        'x_prompt': nrm(ks[2], (BATCH, SEQ, D_MODEL), f32),
        'x_sample': nrm(ks[3], (DEC_BATCH, DEC_SEQ, D_MODEL), f32),
        'cache_cmp_kv': nrm(ks[4], (DEPTH, n_pool, PAGE_SIZE, 2, N_KV, HEAD_DIM), f32),
        'cache_slc_kv': nrm(ks[5], (DEPTH, n_pool, PAGE_SIZE, 2, N_KV, HEAD_DIM), f32),
        'state_win_kv': nrm(ks[6], (DEPTH, DEC_BATCH, min(WINDOW, PAST_LEN), 2, N_KV, HEAD_DIM), f32),
        'state_rnn_h': 0.5 * nrm(ks[7], (DEPTH, DEC_BATCH, D_RNN), f32),
        'state_rnn_conv': nrm(ks[8], (DEPTH, DEC_BATCH, CONV_W - 1, D_RNN), f32),
        'page_table': page_table,
        'p_prompt': nrm(ks[9], (DEPTH, BATCH, SEQ, PLE_DIM), f32),
        'p_sample': nrm(ks[10], (DEPTH, DEC_BATCH, DEC_SEQ, PLE_DIM), f32),
        'g_norm': 1.0 + 0.01 * nrm(ks[11], (DEPTH, D_MODEL), f32),
        'w_in': nrm(ks[12], (DEPTH, D_MODEL, D_IN), f32) * D_MODEL ** -0.5,
        'w_conv': nrm(ks[13], (DEPTH, CONV_W, D_RNN), f32) * CONV_W ** -0.5,
        'b_conv': 0.01 * nrm(ks[14], (DEPTH, D_RNN), f32),
        'w_rg': nrm(ks[15], (DEPTH, 2, RNN_BLOCKS, RNN_BLK, RNN_BLK), f32) * RNN_BLK ** -0.5,
        'b_rg': 0.01 * nrm(ks[16], (DEPTH, 2, D_RNN), f32),
        'lam': lam,
        'w_cmp1': nrm(ks[17], (DEPTH, 2, L_CMP, HEAD_DIM, CMP_HIDDEN), f32) * (L_CMP * HEAD_DIM) ** -0.5,
        'w_cmp2': nrm(ks[18], (DEPTH, 2, CMP_HIDDEN, HEAD_DIM), f32) * CMP_HIDDEN ** -0.5,
        'pe_cmp': 0.1 * nrm(ks[19], (DEPTH, 2, L_CMP, HEAD_DIM), f32),
        'w_rnn_proj': nrm(ks[20], (DEPTH, D_RNN, D_MODEL), f32) * D_RNN ** -0.5,
        'w_nsa_proj': nrm(ks[21], (DEPTH, N_HEADS * HEAD_DIM, D_MODEL), f32) * (N_HEADS * HEAD_DIM) ** -0.5,
        'w_out': nrm(ks[22], (DEPTH, D_MODEL, D_MODEL), f32) * D_MODEL ** -0.5,
        'w_ple': nrm(ks[23], (DEPTH, PLE_DIM, D_MODEL), f32) * PLE_DIM ** -0.5,
        'w_ple_gate': nrm(ks[24], (DEPTH, D_MODEL, D_MODEL), f32) * D_MODEL ** -0.5,
        'g_final': 1.0 + 0.01 * nrm(ks[25], (D_MODEL,), f32),
    }


def reference(x_prompt, x_sample, cache_cmp_kv, cache_slc_kv, state_win_kv, state_rnn_h, state_rnn_conv,
              page_table, p_prompt, p_sample, g_norm, w_in, w_conv, b_conv, w_rg, b_rg, lam,
              w_cmp1, w_cmp2, pe_cmp, w_rnn_proj, w_nsa_proj, w_out, w_ple, w_ple_gate, g_final):
    past = page_table.shape[1] * PAGE_SIZE
    pos_p = jnp.arange(x_prompt.shape[1])
    pos_s = past + jnp.arange(x_sample.shape[1])
    conv0 = jnp.zeros((x_prompt.shape[0], CONV_W - 1, D_RNN), x_prompt.dtype)
    h0 = jnp.zeros((x_prompt.shape[0], D_RNN), x_prompt.dtype)
    xp, xs = x_prompt, x_sample
    cmp_p, cmp_s, slc_p, slc_s, win_p, win_s = [], [], [], [], [], []
    conv_p, conv_s, h_p, h_s = [], [], [], []
    for l in range(DEPTH):
        lw = (g_norm[l], w_in[l], w_conv[l], b_conv[l], w_rg[l], b_rg[l], lam[l], w_cmp1[l], w_cmp2[l],
              pe_cmp[l], w_rnn_proj[l], w_nsa_proj[l], w_out[l], w_ple[l], w_ple_gate[l])
        xp, c1, s1, w1_, v1, hh1 = _layer(xp, p_prompt[l], pos_p, conv0, h0, _nsa_prompt, *lw)
        nsa_s = functools.partial(_nsa_sample, cache_cmp_kv, cache_slc_kv, l, page_table, state_win_kv[l])
        xs, c2, s2, w2_, v2, hh2 = _layer(xs, p_sample[l], pos_s, state_rnn_conv[l], state_rnn_h[l], nsa_s, *lw)
        cmp_p.append(c1); slc_p.append(s1); win_p.append(w1_); conv_p.append(v1); h_p.append(hh1)
        cmp_s.append(c2); slc_s.append(s2); win_s.append(w2_); conv_s.append(v2); h_s.append(hh2)
    y_prompt = _rmsnorm(xp, g_final)
    y_sample = _rmsnorm(xs, g_final)
    return (y_prompt, y_sample, jnp.stack(cmp_p), jnp.stack(cmp_s), jnp.stack(slc_p), jnp.stack(slc_s),
            jnp.stack(win_p), jnp.stack(win_s), jnp.stack(h_p), jnp.stack(h_s), jnp.stack(conv_p), jnp.stack(conv_s))
```

```python
import functools
import math

import jax
import jax.numpy as jnp
from jax import lax
from jax.experimental import pallas as pl
from jax.experimental.pallas import tpu as pltpu

f32 = jnp.float32
bf16 = jnp.bfloat16

D_MODEL = 2048
D_RNN = 2048
RNN_BLOCKS = 8
RNN_BLK = D_RNN // RNN_BLOCKS
CONV_W = 4
C_SCALE = 8.0
N_HEADS = 16
HEAD_DIM = 128
N_KV = 4
GRP = N_HEADS // N_KV
KV_W = N_KV * HEAD_DIM
L_CMP = 32
L_SLC = 64
PAGE_SIZE = 128
N_SEL = 16
N_LOCAL = 2
WINDOW = 512
FORCE = 1e4
SCALE = HEAD_DIM ** -0.5
ROPE_THETA = 10000.0
PLE_DIM = 256
EPS = 1e-6
NEG = -1e30

COL_Q = 0
COL_XR = 2048
COL_GR = 4096
COL_GN = 6144
COL_MG = 8192
COL_KV = 12288
COL_BG = 15360
D_Z = 15872
TN = 512
MIB = 1024 * 1024


def _params(sem, vmem_mib):
    return pltpu.CompilerParams(dimension_semantics=sem, vmem_limit_bytes=vmem_mib * MIB)


def _proj_kernel(x_ref, g_ref, w_ref, cos_ref, sin_ref, o_ref, hn_ref):
    j = pl.program_id(1)

    @pl.when(j == 0)
    def _():
        x = x_ref[...]
        ms = jnp.mean(x * x, axis=-1, keepdims=True)
        hn_ref[...] = (x * lax.rsqrt(ms + EPS) * g_ref[...]).astype(bf16)

    acc = jnp.dot(hn_ref[...], w_ref[...], preferred_element_type=f32)
    jk = COL_KV // TN
    is_rope = (j < COL_XR // TN) | ((j >= jk) & (j < jk + 6) & (j % 2 == 0))

    @pl.when(is_rope)
    def _():
        c = cos_ref[...]
        s = sin_ref[...]
        for h in range(TN // HEAD_DIM):
            a = acc[:, h * HEAD_DIM:(h + 1) * HEAD_DIM]
            o_ref[:, h * HEAD_DIM:(h + 1) * HEAD_DIM] = a * c + pltpu.roll(a, HEAD_DIM // 2, 1) * s

    @pl.when(jnp.logical_not(is_rope))
    def _():
        o_ref[...] = acc


def _in_proj(x, g_norm, w_perm, cos_t, sin_t):
    T = x.shape[0]
    tm = min(T, 1024)
    return pl.pallas_call(
        _proj_kernel,
        grid=(T // tm, D_Z // TN),
        in_specs=[
            pl.BlockSpec((tm, D_MODEL), lambda i, j: (i, 0)),
            pl.BlockSpec((1, D_MODEL), lambda i, j: (0, 0)),
            pl.BlockSpec((D_MODEL, TN), lambda i, j: (0, j)),
            pl.BlockSpec((tm, HEAD_DIM), lambda i, j: (i, 0)),
            pl.BlockSpec((tm, HEAD_DIM), lambda i, j: (i, 0)),
        ],
        out_specs=pl.BlockSpec((tm, TN), lambda i, j: (i, j)),
        out_shape=jax.ShapeDtypeStruct((T, D_Z), f32),
        scratch_shapes=[pltpu.VMEM((tm, D_MODEL), bf16)],
        compiler_params=_params(("arbitrary", "arbitrary"), 48),
        name="in_proj",
    )(x, g_norm, w_perm, cos_t, sin_t)


def _rope_tables(pos):
    half = HEAD_DIM // 2
    inv = ROPE_THETA ** (-jnp.arange(half, dtype=f32) / half)
    ang = pos.astype(f32)[:, None] * inv[None, :]
    cos = jnp.cos(ang)
    sin = jnp.sin(ang)
    return jnp.concatenate([cos, cos], axis=1), jnp.concatenate([-sin, sin], axis=1)


def _permute_w_in(w):
    pad = jnp.zeros((D_MODEL, D_Z - COL_BG - 3 * N_HEADS), w.dtype)
    return jnp.concatenate(
        [w[:, 4096:6144], w[:, 0:4096], w[:, 9216:11264], w[:, 11312:15408], w[:, 6144:9216],
         w[:, 11264:11312], pad], axis=1).astype(bf16)


def _expm1(x):
    u = jnp.exp(x)
    near = jnp.where(u == 1.0, x, (u - 1.0) * x / jnp.log(u))
    return jnp.where(x < -0.5, u - 1.0, near)


def _rglru_core(prompt, xr_ref, gr_ref, st_ref, h0_ref, wconv_ref, bconv_ref, wrg_ref, brg_ref, c_ref,
                wproj_ref, y_ref, hout_ref, a_s, b_s, tail_s, hc_s):
    tm = xr_ref.shape[0]
    groups = tm // 8
    if prompt:
        @pl.when(pl.program_id(0) == 0)
        def _():
            tail_s[...] = jnp.zeros_like(tail_s)
            hc_s[...] = jnp.zeros_like(hc_s)

    row = lax.broadcasted_iota(jnp.int32, (tm, RNN_BLK), 0)
    t8 = row % 8
    tseq = row if prompt else t8
    for n in range(RNN_BLOCKS):
        cols = slice(n * RNN_BLK, (n + 1) * RNN_BLK)
        xr = xr_ref[:, cols]
        xc = jnp.broadcast_to(bconv_ref[:, cols], (tm, RNN_BLK))
        for k in range(CONV_W):
            s = CONV_W - 1 - k
            if s == 0:
                term = xr
            else:
                if prompt:
                    hist = jnp.tile(pltpu.roll(tail_s[:, cols], s, 0), (groups, 1))
                else:
                    hist = pltpu.roll(st_ref[:, cols], tm - 8 + s, 0)
                term = jnp.where(tseq >= s, pltpu.roll(xr, s, 0), hist)
            xc = xc + term * wconv_ref[k:k + 1, cols]
        g = jnp.dot(xc.astype(bf16), wrg_ref[n], preferred_element_type=f32)
        r = jax.nn.sigmoid(g[:, :RNN_BLK] + brg_ref[0:1, cols])
        i = jax.nn.sigmoid(g[:, RNN_BLK:] + brg_ref[1:2, cols])
        log_a = c_ref[:, cols] * r
        a = jnp.exp(log_a)
        b = jnp.sqrt(-_expm1(2.0 * log_a)) * (i * xc)
        for s in (1, 2, 4):
            m = t8 >= s
            a_sh = pltpu.roll(a, s, 0)
            b_sh = pltpu.roll(b, s, 0)
            b = jnp.where(m, a * b_sh + b, b)
            a = jnp.where(m, a * a_sh, a)
        a_s[:, cols] = a
        b_s[:, cols] = b

    if prompt:
        tail_s[...] = xr_ref[tm - 8:tm, :]

        def body(c, h):
            r0 = pl.multiple_of(c * 8, 8)
            hg = a_s[pl.ds(r0, 8), :] * h + b_s[pl.ds(r0, 8), :]
            b_s[pl.ds(r0, 8), :] = hg
            return jnp.broadcast_to(hg[7:8, :], (8, D_RNN))

        h = lax.fori_loop(0, groups, body, hc_s[...])
        hc_s[...] = h
        hout_ref[...] = h
    else:
        b_s[...] = a_s[...] * h0_ref[...] + b_s[...]
        hout_ref[...] = b_s[...]

    gr = gr_ref[...]
    u = (b_s[...] * (gr * jax.nn.sigmoid(gr))).astype(bf16)
    y_ref[...] = jnp.dot(u, wproj_ref[...], preferred_element_type=f32)


def _rglru_prompt_kernel(xr_ref, gr_ref, wconv_ref, bconv_ref, wrg_ref, brg_ref, c_ref, wproj_ref,
                         y_ref, hout_ref, a_s, b_s, tail_s, hc_s):
    _rglru_core(True, xr_ref, gr_ref, None, None, wconv_ref, bconv_ref, wrg_ref, brg_ref, c_ref,
                wproj_ref, y_ref, hout_ref, a_s, b_s, tail_s, hc_s)


def _rglru_sample_kernel(xr_ref, gr_ref, st_ref, h0_ref, wconv_ref, bconv_ref, wrg_ref, brg_ref, c_ref,
                         wproj_ref, y_ref, hout_ref, a_s, b_s):
    _rglru_core(False, xr_ref, gr_ref, st_ref, h0_ref, wconv_ref, bconv_ref, wrg_ref, brg_ref, c_ref,
                wproj_ref, y_ref, hout_ref, a_s, b_s, None, None)


def _const_spec(shape):
    nd = len(shape)
    return pl.BlockSpec(shape, lambda *_: (0,) * nd)


def _rglru_weights(w_conv, b_conv, w_rg, b_rg, lam, w_rnn_proj):
    wrg = jnp.concatenate([w_rg[0], w_rg[1]], axis=-1).astype(bf16)
    c = (-C_SCALE * jax.nn.softplus(-lam.astype(f32))).reshape(1, D_RNN)
    return (w_conv, b_conv.reshape(1, D_RNN), wrg, b_rg, c, w_rnn_proj.astype(bf16))


def _rglru_wspecs():
    return [_const_spec((CONV_W, D_RNN)), _const_spec((1, D_RNN)), _const_spec((RNN_BLOCKS, RNN_BLK, 2 * RNN_BLK)),
            _const_spec((2, D_RNN)), _const_spec((1, D_RNN)), _const_spec((D_RNN, D_MODEL))]


def _rglru_prompt(z, weights):
    T = z.shape[0]
    tm = 256
    return pl.pallas_call(
        _rglru_prompt_kernel,
        grid=(T // tm,),
        in_specs=[pl.BlockSpec((tm, D_RNN), lambda i: (i, COL_XR // D_RNN)),
                  pl.BlockSpec((tm, D_RNN), lambda i: (i, COL_GR // D_RNN))] + _rglru_wspecs(),
        out_specs=[pl.BlockSpec((tm, D_MODEL), lambda i: (i, 0)), _const_spec((8, D_RNN))],
        out_shape=[jax.ShapeDtypeStruct((T, D_MODEL), f32), jax.ShapeDtypeStruct((8, D_RNN), f32)],
        scratch_shapes=[pltpu.VMEM((tm, D_RNN), f32), pltpu.VMEM((tm, D_RNN), f32),
                        pltpu.VMEM((8, D_RNN), f32), pltpu.VMEM((8, D_RNN), f32)],
        compiler_params=_params(("arbitrary",), 56),
        name="rglru_prompt",
    )(z, z, *weights)


def _rglru_sample(z, stpad, h0x, weights):
    T = z.shape[0]
    return pl.pallas_call(
        _rglru_sample_kernel,
        grid=(1,),
        in_specs=[pl.BlockSpec((T, D_RNN), lambda i: (0, COL_XR // D_RNN)),
                  pl.BlockSpec((T, D_RNN), lambda i: (0, COL_GR // D_RNN)),
                  _const_spec((T, D_RNN)), _const_spec((T, D_RNN))] + _rglru_wspecs(),
        out_specs=[_const_spec((T, D_MODEL)), _const_spec((T, D_RNN))],
        out_shape=[jax.ShapeDtypeStruct((T, D_MODEL), f32), jax.ShapeDtypeStruct((T, D_RNN), f32)],
        scratch_shapes=[pltpu.VMEM((T, D_RNN), f32), pltpu.VMEM((T, D_RNN), f32)],
        compiler_params=_params(("arbitrary",), 56),
        name="rglru_sample",
    )(z, z, stpad, h0x, *weights)


def _compress_mlp(load_rows, pe_ref, w1_ref, w2_ref, e):
    acc = None
    for lp in range(L_CMP // 2):
        parts = []
        for dl in range(2):
            l = 2 * lp + dl
            parts.append((load_rows(l) + pe_ref[e, l:l + 1, :]).astype(bf16))
        d = jnp.dot(jnp.concatenate(parts, axis=1), w1_ref[e, lp], preferred_element_type=f32)
        acc = d if acc is None else acc + d
    hid = acc * jax.nn.sigmoid(acc)
    return jnp.dot(hid.astype(bf16), w2_ref[e], preferred_element_type=f32)


def _compress_prompt_kernel(x_ref, pe_ref, w1_ref, w2_ref, o_ref):
    e = pl.program_id(0)
    nb = o_ref.shape[3]

    def load_rows(l):
        return jnp.concatenate([x_ref[pl.ds(l, nb, stride=2 * L_CMP), :],
                                x_ref[pl.ds(L_CMP + l, nb, stride=2 * L_CMP), :]], axis=0)

    res = _compress_mlp(load_rows, pe_ref, w1_ref, w2_ref, e)
    o_ref[0, 0, 0] = res[:nb]
    o_ref[0, 0, 1] = res[nb:]


def _cmp_weights(w_cmp1, w_cmp2, pe_cmp):
    w1 = w_cmp1.reshape(2, L_CMP // 2, 2 * HEAD_DIM, HEAD_DIM).astype(bf16)
    return pe_cmp, w1, w_cmp2.astype(bf16)


def _cmp_wspecs():
    return [_const_spec((2, L_CMP, HEAD_DIM)), _const_spec((2, L_CMP // 2, 2 * HEAD_DIM, HEAD_DIM)),
            _const_spec((2, HEAD_DIM, HEAD_DIM))]


def _compress_prompt(z, cw):
    T = z.shape[0]
    nb = T // (2 * L_CMP)
    col0 = COL_KV // HEAD_DIM
    return pl.pallas_call(
        _compress_prompt_kernel,
        grid=(2, N_KV),
        in_specs=[pl.BlockSpec((T, HEAD_DIM), lambda e, g: (0, col0 + e * N_KV + g))] + _cmp_wspecs(),
        out_specs=pl.BlockSpec((1, 1, 2, nb, HEAD_DIM), lambda e, g: (e, g, 0, 0, 0)),
        out_shape=jax.ShapeDtypeStruct((2, N_KV, 2, nb, HEAD_DIM), f32),
        compiler_params=_params(("arbitrary", "arbitrary"), 32),
        name="compress_prompt",
    )(z, *cw)


CMP_PAGES = 16


def _compress_sample_kernel(pt_ref, cache_ref, pe_ref, w1_ref, w2_ref, o_ref, buf, sem):
    b = pl.program_id(0)
    c = pl.program_id(1)
    nchunk = pl.num_programs(1)
    step = b * nchunk + c
    total = pl.num_programs(0) * nchunk
    slot = step % 2
    half = CMP_PAGES * PAGE_SIZE // (2 * L_CMP)

    def copies(bb, cc, sl):
        out = []
        for p in range(CMP_PAGES):
            page = pt_ref[bb, cc * CMP_PAGES + p]
            for eg in range(2 * N_KV):
                out.append(pltpu.make_async_copy(cache_ref.at[page, :, eg, :],
                                                 buf.at[sl, eg, pl.ds(p * PAGE_SIZE, PAGE_SIZE), :], sem.at[sl]))
        return out

    @pl.when(step == 0)
    def _():
        for cp in copies(b, c, slot):
            cp.start()

    @pl.when(step + 1 < total)
    def _():
        nxt = step + 1
        for cp in copies(nxt // nchunk, nxt % nchunk, 1 - slot):
            cp.start()

    for cp in copies(b, c, slot):
        cp.wait()

    for e in range(2):
        def load_rows(l, e=e):
            parts = []
            for g in range(N_KV):
                for par in range(2):
                    parts.append(buf[slot, e * N_KV + g, pl.ds(par * L_CMP + l, half, stride=2 * L_CMP), :])
            return jnp.concatenate(parts, axis=0)

        res = _compress_mlp(load_rows, pe_ref, w1_ref, w2_ref, e)
        for g in range(N_KV):
            for par in range(2):
                r0 = (g * 2 + par) * half
                o_ref[0, e, g, par] = res[r0:r0 + half]


def _compress_sample(page_table, cache, cw):
    Bs, n_pages = page_table.shape
    n_pool = cache.shape[0]
    nchunk = n_pages // CMP_PAGES
    half = CMP_PAGES * PAGE_SIZE // (2 * L_CMP)
    nb = nchunk * half
    grid_spec = pltpu.PrefetchScalarGridSpec(
        num_scalar_prefetch=1,
        grid=(Bs, nchunk),
        in_specs=[pl.BlockSpec(memory_space=pl.ANY)] + [
            pl.BlockSpec(s.block_shape, lambda b, c, pt, n=len(s.block_shape): (0,) * n) for s in _cmp_wspecs()],
        out_specs=pl.BlockSpec((1, 2, N_KV, 2, half, HEAD_DIM), lambda b, c, pt: (b, 0, 0, 0, c, 0)),
        scratch_shapes=[pltpu.VMEM((2, 2 * N_KV, CMP_PAGES * PAGE_SIZE, HEAD_DIM), f32),
                        pltpu.SemaphoreType.DMA((2,))],
    )
    return pl.pallas_call(
        _compress_sample_kernel,
        grid_spec=grid_spec,
        out_shape=jax.ShapeDtypeStruct((Bs, 2, N_KV, 2, nb, HEAD_DIM), f32),
        compiler_params=_params(("arbitrary", "arbitrary"), 40),
        name="compress_sample",
    )(page_table, cache.reshape(n_pool, PAGE_SIZE, 2 * N_KV, HEAD_DIM), *cw)


def _cmp_attn_kernel(pos_base, pos_step, n_blk, as_bias, q_ref, ckv_ref, o_ref, sel_ref):
    tq = q_ref.shape[0]
    nb = ckv_ref.shape[4]
    nc = 2 * nb
    width = sel_ref.shape[-1] if as_bias else pl.cdiv(n_blk, HEAD_DIM) * HEAD_DIM
    m_rows = GRP * tq
    pos0 = pos_base + pl.program_id(0) * pos_step
    pos_col = pos0 + lax.broadcasted_iota(jnp.int32, (tq, 1), 0)
    pos_rows = jnp.concatenate([pos_col] * GRP, axis=0)
    col = lax.broadcasted_iota(jnp.int32, (m_rows, nc), 1)
    c_idx = 2 * (col % nb) + col // nb
    readable = ((c_idx + 1) * L_CMP - 1) <= pos_rows

    lane = lax.broadcasted_iota(jnp.int32, (tq, width), 1).astype(f32)
    jt = (pos_col // L_SLC).astype(f32)
    forced = (lane == 0.0) | ((lane >= jt - (N_LOCAL - 1)) & (lane <= jt))
    future = lane > jt
    exists = lane < float(n_blk)
    out_lane = lax.broadcasted_iota(jnp.int32, (tq, HEAD_DIM), 1)

    for g in range(N_KV):
        ck = ckv_ref[0, 0, g].reshape(nc, HEAD_DIM).astype(bf16)
        cv = ckv_ref[0, 1, g].reshape(nc, HEAD_DIM).astype(bf16)
        qg = jnp.concatenate([q_ref[:, (g * GRP + r) * HEAD_DIM:(g * GRP + r + 1) * HEAD_DIM]
                              for r in range(GRP)], axis=0).astype(bf16)
        s = lax.dot_general(qg, ck, (((1,), (1,)), ((), ())), preferred_element_type=f32) * SCALE
        s = jnp.where(readable, s, -jnp.inf)
        mx = jnp.max(s, axis=-1, keepdims=True)
        mx = jnp.where(mx > -jnp.inf, mx, 0.0)
        ex = jnp.where(readable, jnp.exp(s - mx), 0.0)
        p = ex / jnp.maximum(jnp.sum(ex, axis=-1, keepdims=True), 1e-30)
        o = jnp.dot(p.astype(bf16), cv, preferred_element_type=f32)
        for r in range(GRP):
            o_ref[:, (g * GRP + r) * HEAD_DIM:(g * GRP + r + 1) * HEAD_DIM] = o[r * tq:(r + 1) * tq]

        pr = p[0:tq]
        for r in range(1, GRP):
            pr = pr + p[r * tq:(r + 1) * tq]
        ps = pr[:, :nb] + pr[:, nb:]
        if width > nb:
            ps = jnp.concatenate([ps, jnp.zeros((tq, width - nb), f32)], axis=1)
        score = jnp.where(future, -jnp.inf, jnp.where(forced, FORCE, ps))
        removed = jnp.logical_not(exists)
        idx_out = jnp.zeros((tq, HEAD_DIM), jnp.int32)
        for k in range(N_SEL):
            se = jnp.where(removed, -jnp.inf, score)
            top = jnp.max(se, axis=-1, keepdims=True)
            cand = jnp.where((se == top) & jnp.logical_not(removed), lane, float(width))
            idx = jnp.min(cand, axis=-1, keepdims=True)
            removed = removed | (lane == idx)
            idx_out = jnp.where(out_lane == k, idx.astype(jnp.int32), idx_out)
        if as_bias:
            sel_ref[g] = jnp.where(removed & exists, 0.0, NEG).astype(bf16)
        else:
            sel_ref[:, g * HEAD_DIM:(g + 1) * HEAD_DIM] = idx_out


def _cmp_attn(z, ckv, tq, pos_base, pos_step, n_blk, as_bias):
    T = z.shape[0]
    nb = ckv.shape[4]
    per_b = 0 if ckv.shape[0] == 1 else 1
    if as_bias:
        sel_spec = pl.BlockSpec((N_KV, tq, n_blk), lambda i: (0, i, 0))
        sel_shape = jax.ShapeDtypeStruct((N_KV, T, n_blk), bf16)
    else:
        sel_spec = pl.BlockSpec((tq, N_KV * HEAD_DIM), lambda i: (i, 0))
        sel_shape = jax.ShapeDtypeStruct((T, N_KV * HEAD_DIM), jnp.int32)
    kern = functools.partial(_cmp_attn_kernel, pos_base, pos_step, n_blk, as_bias)
    return pl.pallas_call(
        kern,
        grid=(T // tq,),
        in_specs=[pl.BlockSpec((tq, N_HEADS * HEAD_DIM), lambda i: (i, COL_Q)),
                  pl.BlockSpec((1, 2, N_KV, 2, nb, HEAD_DIM), lambda i: (i * per_b, 0, 0, 0, 0, 0))],
        out_specs=[pl.BlockSpec((tq, N_HEADS * HEAD_DIM), lambda i: (i, 0)), sel_spec],
        out_shape=[jax.ShapeDtypeStruct((T, N_HEADS * HEAD_DIM), f32), sel_shape],
        compiler_params=_params(("arbitrary",), 40),
        name="cmp_attn",
    )(z, ckv)


FLASH_T = 256
M_INIT = -3e38


def _flash_kernel(windowed, q_ref, k_ref, v_ref, *rest):
    if windowed:
        o_ref, kb_s, vb_s, qa_s = rest
    else:
        bias_ref, oh_ref, o_ref, kb_s, vb_s, qa_s = rest
    tq = q_ref.shape[0]
    tk = tq
    T = k_ref.shape[0]
    m_rows = GRP * tq
    qi = pl.program_id(1)

    @pl.when(qi == 0)
    def _():
        def cast(c, _):
            r0 = pl.multiple_of(c * tk, tk)
            kb_s[pl.ds(r0, tk), 0:HEAD_DIM] = k_ref[pl.ds(r0, tk), :].astype(bf16)
            if not windowed:
                kb_s[pl.ds(r0, tk), HEAD_DIM:2 * HEAD_DIM] = oh_ref[pl.ds(r0, tk), :]
            vb_s[pl.ds(r0, tk), :] = v_ref[pl.ds(r0, tk), :].astype(bf16)
            return 0
        lax.fori_loop(0, T // tk, cast, 0)

    for r in range(GRP):
        qa_s[r * tq:(r + 1) * tq, 0:HEAD_DIM] = (q_ref[:, r * HEAD_DIM:(r + 1) * HEAD_DIM] * SCALE).astype(bf16)
        if not windowed:
            qa_s[r * tq:(r + 1) * tq, HEAD_DIM:2 * HEAD_DIM] = bias_ref[0]

    rel = (lax.broadcasted_iota(jnp.int32, (m_rows, tk), 0) % tq) - lax.broadcasted_iota(jnp.int32, (m_rows, tk), 1)

    def step(kt, carry, masked):
        m, l, acc = carry
        k0 = pl.multiple_of(kt * tk, tk)
        s = lax.dot_general(qa_s[...], kb_s[pl.ds(k0, tk), :], (((1,), (1,)), ((), ())),
                            preferred_element_type=f32)
        if masked:
            d = rel + (qi - kt) * tk
            ok = (d >= 0) & (d <= WINDOW) if windowed else d >= 0
            s = jnp.where(ok, s, NEG)
        m_new = jnp.maximum(m, jnp.max(s, axis=-1, keepdims=True))
        alpha = jnp.exp(m - m_new)
        p = jnp.exp(s - m_new)
        l = alpha * l + jnp.sum(p, axis=-1, keepdims=True)
        acc = alpha * acc + jnp.dot(p.astype(bf16), vb_s[pl.ds(k0, tk), :], preferred_element_type=f32)
        return m_new, l, acc

    carry = (jnp.full((m_rows, 1), M_INIT, f32), jnp.zeros((m_rows, 1), f32), jnp.zeros((m_rows, HEAD_DIM), f32))
    if windowed:
        lo = jnp.maximum(qi - WINDOW // tk, 0)
        carry = lax.fori_loop(lo, qi + 1, functools.partial(step, masked=True), carry)
    else:
        carry = lax.fori_loop(0, qi, functools.partial(step, masked=False), carry)
        carry = step(qi, carry, True)
    _, l, acc = carry
    o = acc / l
    for r in range(GRP):
        o_ref[:, r * HEAD_DIM:(r + 1) * HEAD_DIM] = o[r * tq:(r + 1) * tq]


def _flash_prompt(z, branch, bias=None):
    T = z.shape[0]
    tq = FLASH_T
    windowed = bias is None
    kcol = COL_KV // HEAD_DIM + branch * 2 * N_KV
    in_specs = [pl.BlockSpec((tq, GRP * HEAD_DIM), lambda g, i: (i, g)),
                pl.BlockSpec((T, HEAD_DIM), lambda g, i: (0, kcol + g)),
                pl.BlockSpec((T, HEAD_DIM), lambda g, i: (0, kcol + N_KV + g))]
    args = [z, z, z]
    kw = HEAD_DIM
    if not windowed:
        n_blk = bias.shape[-1]
        onehot = (jnp.arange(T)[:, None] // L_SLC == jnp.arange(n_blk)[None, :]).astype(bf16)
        in_specs += [pl.BlockSpec((1, tq, n_blk), lambda g, i: (g, i, 0)), _const_spec((T, n_blk))]
        args += [bias, onehot]
        kw = HEAD_DIM + n_blk
    return pl.pallas_call(
        functools.partial(_flash_kernel, windowed),
        grid=(N_KV, T // tq),
        in_specs=in_specs,
        out_specs=pl.BlockSpec((tq, GRP * HEAD_DIM), lambda g, i: (i, g)),
        out_shape=jax.ShapeDtypeStruct((T, N_HEADS * HEAD_DIM), f32),
        scratch_shapes=[pltpu.VMEM((T, kw), bf16), pltpu.VMEM((T, HEAD_DIM), bf16),
                        pltpu.VMEM((GRP * tq, kw), bf16)],
        compiler_params=_params(("arbitrary", "arbitrary"), 48),
        name="flash_win" if windowed else "flash_slc",
    )(*args)


def _slc_sample_kernel(nb_past, pos_base, idx_ref, pt_ref, q_ref, pool_ref, newb_ref, o_ref, kbuf, vbuf, sem):
    b = pl.program_id(0)
    g = pl.program_id(1)
    ts = q_ref.shape[0]
    n_pages = pt_ref.shape[1]
    nbn = newb_ref.shape[1] // L_SLC
    blocks_per_page = PAGE_SIZE // L_SLC
    n_copy = ts * N_SEL

    def block_id(c):
        t = c // N_SEL
        return idx_ref[((b * ts + t) * N_KV + g) * N_SEL + c % N_SEL]

    def issue(c, _):
        t = c // N_SEL
        n = c % N_SEL
        bid = block_id(c)
        dst_k = kbuf.at[t, pl.ds(pl.multiple_of(n * L_SLC, L_SLC), L_SLC), :]
        dst_v = vbuf.at[t, pl.ds(pl.multiple_of(n * L_SLC, L_SLC), L_SLC), :]

        @pl.when(bid < nb_past)
        def _():
            page = pt_ref[b, jnp.minimum(bid // blocks_per_page, n_pages - 1)]
            rows = pl.ds(pl.multiple_of((bid % blocks_per_page) * L_SLC, L_SLC), L_SLC)
            pltpu.make_async_copy(pool_ref.at[page, rows, g, :], dst_k, sem.at[0]).start()
            pltpu.make_async_copy(pool_ref.at[page, rows, N_KV + g, :], dst_v, sem.at[0]).start()

        @pl.when(bid >= nb_past)
        def _():
            nid = jnp.clip(bid - nb_past, 0, nbn - 1)
            rows = pl.ds(pl.multiple_of(nid * L_SLC, L_SLC), L_SLC)
            pltpu.make_async_copy(newb_ref.at[b, rows, g, :], dst_k, sem.at[0]).start()
            pltpu.make_async_copy(newb_ref.at[b, rows, N_KV + g, :], dst_v, sem.at[0]).start()
        return 0

    lax.fori_loop(0, n_copy, issue, 0)

    def drain(c, _):
        pltpu.make_async_copy(newb_ref.at[0, pl.ds(0, L_SLC), 0, :], kbuf.at[0, pl.ds(0, L_SLC), :], sem.at[0]).wait()
        pltpu.make_async_copy(newb_ref.at[0, pl.ds(0, L_SLC), 0, :], vbuf.at[0, pl.ds(0, L_SLC), :], sem.at[0]).wait()
        return 0

    lax.fori_loop(0, n_copy, drain, 0)

    lane = lax.broadcasted_iota(jnp.int32, (1, N_SEL * L_SLC), 1)
    for t in range(ts):
        pos = pos_base + t
        kpos = jnp.zeros((1, N_SEL * L_SLC), jnp.int32)
        for n in range(N_SEL):
            bid = block_id(t * N_SEL + n)
            kpos = jnp.where(lane // L_SLC == n, bid * L_SLC + lane % L_SLC, kpos)
        ok = kpos <= pos
        qt = jnp.concatenate([q_ref[t:t + 1, r * HEAD_DIM:(r + 1) * HEAD_DIM] for r in range(GRP)], axis=0)
        s = lax.dot_general(qt.astype(bf16), kbuf[t].astype(bf16), (((1,), (1,)), ((), ())),
                            preferred_element_type=f32) * SCALE
        s = jnp.where(ok, s, -jnp.inf)
        mx = jnp.max(s, axis=-1, keepdims=True)
        mx = jnp.where(mx > -jnp.inf, mx, 0.0)
        ex = jnp.where(ok, jnp.exp(s - mx), 0.0)
        p = ex / jnp.maximum(jnp.sum(ex, axis=-1, keepdims=True), 1e-30)
        o = jnp.dot(p.astype(bf16), vbuf[t].astype(bf16), preferred_element_type=f32)
        for r in range(GRP):
            o_ref[t:t + 1, r * HEAD_DIM:(r + 1) * HEAD_DIM] = o[r:r + 1, :]


def _slc_sample(zs, idx_flat, page_table, pool, newb, ts, pos_base):
    Bs = page_table.shape[0]
    n_pool = pool.shape[0]
    nb_past = page_table.shape[1] * (PAGE_SIZE // L_SLC)
    grid_spec = pltpu.PrefetchScalarGridSpec(
        num_scalar_prefetch=2,
        grid=(Bs, N_KV),
        in_specs=[pl.BlockSpec((ts, GRP * HEAD_DIM), lambda b, g, *_: (b, g)),
                  pl.BlockSpec(memory_space=pl.ANY), pl.BlockSpec(memory_space=pl.ANY)],
        out_specs=pl.BlockSpec((ts, GRP * HEAD_DIM), lambda b, g, *_: (b, g)),
        scratch_shapes=[pltpu.VMEM((ts, N_SEL * L_SLC, HEAD_DIM), f32), pltpu.VMEM((ts, N_SEL * L_SLC, HEAD_DIM), f32),
                        pltpu.SemaphoreType.DMA((1,))],
    )
    return pl.pallas_call(
        functools.partial(_slc_sample_kernel, nb_past, pos_base),
        grid_spec=grid_spec,
        out_shape=jax.ShapeDtypeStruct((Bs * ts, N_HEADS * HEAD_DIM), f32),
        compiler_params=_params(("arbitrary", "arbitrary"), 32),
        name="slc_sample",
    )(idx_flat, page_table, zs, pool.reshape(n_pool, PAGE_SIZE, 2 * N_KV, HEAD_DIM), newb)


def _win_sample_kernel(q_ref, new_ref, win_ref, o_ref, kall, sem):
    b = pl.program_id(0)
    ts = q_ref.shape[0]
    wb = win_ref.shape[1]
    rows = kall.shape[1]

    def copies():
        return [pltpu.make_async_copy(win_ref.at[b, :, eg, :], kall.at[eg, pl.ds(0, wb), :], sem.at[0])
                for eg in range(2 * N_KV)]

    for cp in copies():
        cp.start()
    for eg in range(2 * N_KV):
        kall[eg, wb:wb + ts, :] = new_ref[:, eg * HEAD_DIM:(eg + 1) * HEAD_DIM]
        kall[eg, wb + ts:rows, :] = jnp.zeros((rows - wb - ts, HEAD_DIM), f32)
    for cp in copies():
        cp.wait()

    m_rows = GRP * ts
    tq = lax.broadcasted_iota(jnp.int32, (m_rows, rows), 0) % ts
    ki = lax.broadcasted_iota(jnp.int32, (m_rows, rows), 1)
    d = tq + wb - ki
    ok = (d >= 0) & (d <= WINDOW) & (ki < wb + ts)
    for g in range(N_KV):
        qg = jnp.concatenate([q_ref[:, (g * GRP + r) * HEAD_DIM:(g * GRP + r + 1) * HEAD_DIM]
                              for r in range(GRP)], axis=0).astype(bf16)
        s = lax.dot_general(qg, kall[g].astype(bf16), (((1,), (1,)), ((), ())),
                            preferred_element_type=f32) * SCALE
        s = jnp.where(ok, s, -jnp.inf)
        mx = jnp.max(s, axis=-1, keepdims=True)
        mx = jnp.where(mx > -jnp.inf, mx, 0.0)
        ex = jnp.where(ok, jnp.exp(s - mx), 0.0)
        p = ex / jnp.maximum(jnp.sum(ex, axis=-1, keepdims=True), 1e-30)
        o = jnp.dot(p.astype(bf16), kall[N_KV + g].astype(bf16), preferred_element_type=f32)
        for r in range(GRP):
            o_ref[:, (g * GRP + r) * HEAD_DIM:(g * GRP + r + 1) * HEAD_DIM] = o[r * ts:(r + 1) * ts]


def _win_sample(zs, win_buf, ts):
    Bs, wb = win_buf.shape[0], win_buf.shape[1]
    rows = wb + HEAD_DIM
    ncol = 2 * KV_W
    return pl.pallas_call(
        _win_sample_kernel,
        grid=(Bs,),
        in_specs=[pl.BlockSpec((ts, N_HEADS * HEAD_DIM), lambda b: (b, COL_Q)),
                  pl.BlockSpec((ts, ncol), lambda b: (b, (COL_KV + 2 * ncol) // ncol)),
                  pl.BlockSpec(memory_space=pl.ANY)],
        out_specs=pl.BlockSpec((ts, N_HEADS * HEAD_DIM), lambda b: (b, 0)),
        out_shape=jax.ShapeDtypeStruct((Bs * ts, N_HEADS * HEAD_DIM), f32),
        scratch_shapes=[pltpu.VMEM((2 * N_KV, rows, HEAD_DIM), f32), pltpu.SemaphoreType.DMA((1,))],
        compiler_params=_params(("arbitrary",), 32),
        name="win_sample",
    )(zs, zs, win_buf.reshape(Bs, wb, 2 * N_KV, HEAD_DIM))


def _nsa_out_kernel(oc_ref, os_ref, ow_ref, gn_ref, bg_ref, w_ref, y_ref, u_s):
    bgs = jax.nn.sigmoid(bg_ref[...])
    gn = gn_ref[...]
    for h in range(N_HEADS):
        cols = slice(h * HEAD_DIM, (h + 1) * HEAD_DIM)
        o = (bgs[:, h:h + 1] * oc_ref[:, cols] + bgs[:, N_HEADS + h:N_HEADS + h + 1] * os_ref[:, cols]
             + bgs[:, 2 * N_HEADS + h:2 * N_HEADS + h + 1] * ow_ref[:, cols])
        gh = gn[:, cols]
        u_s[:, cols] = (o * (gh * jax.nn.sigmoid(gh))).astype(bf16)
    y_ref[...] = jnp.dot(u_s[...], w_ref[...], preferred_element_type=f32)


def _nsa_out(o_cmp, o_slc, o_win, z, w):
    T = z.shape[0]
    tm = min(T, 256)
    row = lambda i: (i, 0)
    return pl.pallas_call(
        _nsa_out_kernel,
        grid=(T // tm,),
        in_specs=[pl.BlockSpec((tm, D_MODEL), row)] * 3 + [
            pl.BlockSpec((tm, D_MODEL), lambda i: (i, COL_GN // D_MODEL)),
            pl.BlockSpec((tm, TN), lambda i: (i, COL_BG // TN)),
            _const_spec((D_MODEL, D_MODEL))],
        out_specs=pl.BlockSpec((tm, D_MODEL), row),
        out_shape=jax.ShapeDtypeStruct((T, D_MODEL), f32),
        scratch_shapes=[pltpu.VMEM((tm, D_MODEL), bf16)],
        compiler_params=_params(("arbitrary",), 48),
        name="nsa_out",
    )(o_cmp, o_slc, o_win, z, z, w)


def _merge_kernel(yr_ref, yn_ref, m0_ref, m1_ref, x_ref, w_ref, o_ref):
    u = jax.nn.sigmoid(m0_ref[...]) * yr_ref[...] + jax.nn.sigmoid(m1_ref[...]) * yn_ref[...]
    o_ref[...] = x_ref[...] + jnp.dot(u.astype(bf16), w_ref[...], preferred_element_type=f32)


def _merge(y_rnn, y_nsa, z, x, w):
    T = z.shape[0]
    tm = min(T, 256)
    row = lambda i: (i, 0)
    return pl.pallas_call(
        _merge_kernel,
        grid=(T // tm,),
        in_specs=[pl.BlockSpec((tm, D_MODEL), row)] * 2 + [
            pl.BlockSpec((tm, D_MODEL), lambda i: (i, COL_MG // D_MODEL)),
            pl.BlockSpec((tm, D_MODEL), lambda i: (i, COL_MG // D_MODEL + 1)),
            pl.BlockSpec((tm, D_MODEL), row), _const_spec((D_MODEL, D_MODEL))],
        out_specs=pl.BlockSpec((tm, D_MODEL), row),
        out_shape=jax.ShapeDtypeStruct((T, D_MODEL), f32),
        compiler_params=_params(("arbitrary",), 48),
        name="merge_out",
    )(y_rnn, y_nsa, z, z, x, w)


def _ple_kernel(x_ref, p_ref, wp_ref, wg_ref, gf_ref, o_ref):
    x = x_ref[...]
    emb = jnp.dot(p_ref[...].astype(bf16), wp_ref[...], preferred_element_type=f32)
    gate = jax.nn.sigmoid(jnp.dot(x.astype(bf16), wg_ref[...], preferred_element_type=f32))
    x = x + emb * gate
    ms = jnp.mean(x * x, axis=-1, keepdims=True)
    o_ref[...] = x * lax.rsqrt(ms + EPS) * gf_ref[...]


def _ple_norm(x1, p, w_ple, w_gate, g_final):
    T = x1.shape[0]
    tm = min(T, 256)
    row = lambda i: (i, 0)
    return pl.pallas_call(
        _ple_kernel,
        grid=(T // tm,),
        in_specs=[pl.BlockSpec((tm, D_MODEL), row), pl.BlockSpec((tm, PLE_DIM), row),
                  _const_spec((PLE_DIM, D_MODEL)), _const_spec((D_MODEL, D_MODEL)), _const_spec((1, D_MODEL))],
        out_specs=pl.BlockSpec((tm, D_MODEL), row),
        out_shape=jax.ShapeDtypeStruct((T, D_MODEL), f32),
        compiler_params=_params(("arbitrary",), 40),
        name="ple_norm",
    )(x1, p, w_ple, w_gate, g_final)


def kernel(x_prompt, x_sample, cache_cmp_kv, cache_slc_kv, state_win_kv, state_rnn_h, state_rnn_conv, page_table,
           p_prompt, p_sample, g_norm, w_in, w_conv, b_conv, w_rg, b_rg, lam, w_cmp1, w_cmp2, pe_cmp,
           w_rnn_proj, w_nsa_proj, w_out, w_ple, w_ple_gate, g_final):
    B, T, _ = x_prompt.shape
    Bs, Ts, _ = x_sample.shape
    past = page_table.shape[1] * PAGE_SIZE
    assert B == 1 and w_in.shape[0] == 1, "single prompt sequence, single layer"
    assert Ts == 8 and T % 1024 == 0 and T >= CONV_W - 1 and Ts >= CONV_W - 1 and Ts < L_CMP
    w_perm = _permute_w_in(w_in[0])
    rg_w = _rglru_weights(w_conv[0], b_conv[0], w_rg[0], b_rg[0], lam[0], w_rnn_proj[0])

    cos_p, sin_p = _rope_tables(jnp.arange(T))
    zp = _in_proj(x_prompt.reshape(T, D_MODEL), g_norm, w_perm, cos_p, sin_p)
    cos_s, sin_s = _rope_tables(jnp.tile(past + jnp.arange(Ts), Bs))
    zs = _in_proj(x_sample.reshape(Bs * Ts, D_MODEL), g_norm, w_perm, cos_s, sin_s)

    y_rnn_p, h_p = _rglru_prompt(zp, rg_w)
    stpad = jnp.pad(state_rnn_conv[0], ((0, 0), (8 - (CONV_W - 1), 0), (0, 0))).reshape(Bs * Ts, D_RNN)
    h0x = jnp.repeat(state_rnn_h[0], Ts, axis=0)
    y_rnn_s, hr_s = _rglru_sample(zs, stpad, h0x, rg_w)

    cw = _cmp_weights(w_cmp1[0], w_cmp2[0], pe_cmp[0])
    ckv_p = _compress_prompt(zp, cw)[None]
    ckv_s = _compress_sample(page_table, cache_cmp_kv[0], cw)
    n_blk_p = -(-T // L_SLC)
    n_blk_s = -(-(past + Ts) // L_SLC)
    o_cmp_p, bias_p = _cmp_attn(zp, ckv_p, FLASH_T, 0, FLASH_T, n_blk_p, True)
    o_cmp_s, idx_s = _cmp_attn(zs, ckv_s, Ts, past, 0, n_blk_s, False)
    o_slc_p = _flash_prompt(zp, 1, bias_p)
    o_win_p = _flash_prompt(zp, 2)

    ncol = 2 * KV_W
    nb_past = past // L_SLC
    new_rows = (n_blk_s - nb_past) * L_SLC
    newb = jnp.pad(zs[:, COL_KV + ncol:COL_KV + 2 * ncol].reshape(Bs, Ts, 2 * N_KV, HEAD_DIM),
                   ((0, 0), (0, new_rows - Ts), (0, 0), (0, 0)))
    idx_flat = idx_s.reshape(Bs * Ts, N_KV, HEAD_DIM)[:, :, :N_SEL].reshape(-1)
    o_slc_s = _slc_sample(zs, idx_flat, page_table, cache_slc_kv[0], newb, Ts, past)
    o_win_s = _win_sample(zs, state_win_kv[0], Ts)

    w_nsa_b, w_out_b = w_nsa_proj[0].astype(bf16), w_out[0].astype(bf16)
    w_ple_b, w_gate_b = w_ple[0].astype(bf16), w_ple_gate[0].astype(bf16)
    gf = g_final.reshape(1, D_MODEL)

    def tail(z, x, p, y_rnn, o_cmp, o_slc, o_win):
        y_nsa = _nsa_out(o_cmp, o_slc, o_win, z, w_nsa_b)
        x1 = _merge(y_rnn, y_nsa, z, x, w_out_b)
        return _ple_norm(x1, p, w_ple_b, w_gate_b, gf)

    y_p = tail(zp, x_prompt.reshape(T, D_MODEL), p_prompt.reshape(T, PLE_DIM), y_rnn_p, o_cmp_p, o_slc_p, o_win_p)
    y_s = tail(zs, x_sample.reshape(Bs * Ts, D_MODEL), p_sample.reshape(Bs * Ts, PLE_DIM), y_rnn_s, o_cmp_s, o_slc_s,
               o_win_s)

    kv_shape = (2, N_KV, HEAD_DIM)
    zs3 = zs.reshape(Bs, Ts, D_Z)
    keep_p = min(WINDOW, T)
    win_s = jnp.concatenate([state_win_kv[0], zs3[:, :, COL_KV + 2 * ncol:COL_KV + 3 * ncol].reshape(Bs, Ts, *kv_shape)],
                            axis=1)
    keep_s = min(WINDOW, past + Ts)
    hist = CONV_W - 1
    return (y_p.reshape(1, T, D_MODEL),
            y_s.reshape(Bs, Ts, D_MODEL),
            zp[:, COL_KV:COL_KV + ncol].reshape(1, 1, T, *kv_shape),
            zs3[:, :, COL_KV:COL_KV + ncol].reshape(1, Bs, Ts, *kv_shape),
            zp[:, COL_KV + ncol:COL_KV + 2 * ncol].reshape(1, 1, T, *kv_shape),
            zs3[:, :, COL_KV + ncol:COL_KV + 2 * ncol].reshape(1, Bs, Ts, *kv_shape),
            zp[T - keep_p:, COL_KV + 2 * ncol:COL_KV + 3 * ncol].reshape(1, 1, keep_p, *kv_shape),
            win_s[None, :, win_s.shape[1] - keep_s:],
            h_p[7:8].reshape(1, 1, D_RNN),
            hr_s.reshape(Bs, Ts, D_RNN)[None, :, Ts - 1],
            zp[T - hist:, COL_XR:COL_XR + D_RNN].reshape(1, 1, hist, D_RNN),
            zs3[None, :, Ts - hist:, COL_XR:COL_XR + D_RNN])
```

```python
import functools
import math

import jax
import jax.numpy as jnp
from jax import lax
from jax.experimental import pallas as pl
from jax.experimental.pallas import tpu as pltpu

f32 = jnp.float32
bf16 = jnp.bfloat16

D_MODEL = 2048
D_RNN = 2048
RNN_BLOCKS = 8
RNN_BLK = D_RNN // RNN_BLOCKS
CONV_W = 4
C_SCALE = 8.0
N_HEADS = 16
HEAD_DIM = 128
N_KV = 4
GRP = N_HEADS // N_KV
KV_W = N_KV * HEAD_DIM
L_CMP = 32
L_SLC = 64
PAGE_SIZE = 128
N_SEL = 16
N_LOCAL = 2
WINDOW = 512
FORCE = 1e4
SCALE = HEAD_DIM ** -0.5
ROPE_THETA = 10000.0
PLE_DIM = 256
EPS = 1e-6
NEG = -1e30

COL_Q = 0
COL_XR = 2048
COL_GR = 4096
COL_GN = 6144
COL_MG = 8192
COL_KV = 12288
COL_BG = 15360
D_Z = 15872
TN = 512
MIB = 1024 * 1024


def _params(sem, vmem_mib):
    return pltpu.CompilerParams(dimension_semantics=sem, vmem_limit_bytes=vmem_mib * MIB)


def _proj_kernel(x_ref, g_ref, w_ref, cos_ref, sin_ref, o_ref, hn_ref):
    j = pl.program_id(1)

    @pl.when(j == 0)
    def _():
        x = x_ref[...]
        ms = jnp.mean(x * x, axis=-1, keepdims=True)
        hn_ref[...] = (x * lax.rsqrt(ms + EPS) * g_ref[...]).astype(bf16)

    acc = jnp.dot(hn_ref[...], w_ref[...], preferred_element_type=f32)
    jk = COL_KV // TN
    is_rope = (j < COL_XR // TN) | ((j >= jk) & (j < jk + 6) & (j % 2 == 0))

    @pl.when(is_rope)
    def _():
        c = cos_ref[...]
        s = sin_ref[...]
        for h in range(TN // HEAD_DIM):
            a = acc[:, h * HEAD_DIM:(h + 1) * HEAD_DIM]
            o_ref[:, h * HEAD_DIM:(h + 1) * HEAD_DIM] = a * c + pltpu.roll(a, HEAD_DIM // 2, 1) * s

    @pl.when(jnp.logical_not(is_rope))
    def _():
        o_ref[...] = acc


def _in_proj(x, g_norm, w_perm, cos_t, sin_t):
    T = x.shape[0]
    tm = min(T, 1024)
    return pl.pallas_call(
        _proj_kernel,
        grid=(T // tm, D_Z // TN),
        in_specs=[
            pl.BlockSpec((tm, D_MODEL), lambda i, j: (i, 0)),
            pl.BlockSpec((1, D_MODEL), lambda i, j: (0, 0)),
            pl.BlockSpec((D_MODEL, TN), lambda i, j: (0, j)),
            pl.BlockSpec((tm, HEAD_DIM), lambda i, j: (i, 0)),
            pl.BlockSpec((tm, HEAD_DIM), lambda i, j: (i, 0)),
        ],
        out_specs=pl.BlockSpec((tm, TN), lambda i, j: (i, j)),
        out_shape=jax.ShapeDtypeStruct((T, D_Z), f32),
        scratch_shapes=[pltpu.VMEM((tm, D_MODEL), bf16)],
        compiler_params=_params(("arbitrary", "arbitrary"), 48),
        name="in_proj",
    )(x, g_norm, w_perm, cos_t, sin_t)


def _rope_tables(pos):
    half = HEAD_DIM // 2
    inv = ROPE_THETA ** (-jnp.arange(half, dtype=f32) / half)
    ang = pos.astype(f32)[:, None] * inv[None, :]
    cos = jnp.cos(ang)
    sin = jnp.sin(ang)
    return jnp.concatenate([cos, cos], axis=1), jnp.concatenate([-sin, sin], axis=1)


def _permute_w_in(w):
    pad = jnp.zeros((D_MODEL, D_Z - COL_BG - 3 * N_HEADS), w.dtype)
    return jnp.concatenate(
        [w[:, 4096:6144], w[:, 0:4096], w[:, 9216:11264], w[:, 11312:15408], w[:, 6144:9216],
         w[:, 11264:11312], pad], axis=1).astype(bf16)


def _expm1(x):
    u = jnp.exp(x)
    near = jnp.where(u == 1.0, x, (u - 1.0) * x / jnp.log(u))
    return jnp.where(x < -0.5, u - 1.0, near)


def _rglru_core(prompt, xr_ref, gr_ref, st_ref, h0_ref, wconv_ref, bconv_ref, wrg_ref, brg_ref, c_ref,
                wproj_ref, y_ref, hout_ref, a_s, b_s, tail_s, hc_s):
    tm = xr_ref.shape[0]
    groups = tm // 8
    if prompt:
        @pl.when(pl.program_id(0) == 0)
        def _():
            tail_s[...] = jnp.zeros_like(tail_s)
            hc_s[...] = jnp.zeros_like(hc_s)

    row = lax.broadcasted_iota(jnp.int32, (tm, RNN_BLK), 0)
    t8 = row % 8
    tseq = row if prompt else t8
    for n in range(RNN_BLOCKS):
        cols = slice(n * RNN_BLK, (n + 1) * RNN_BLK)
        xr = xr_ref[:, cols]
        xc = jnp.broadcast_to(bconv_ref[:, cols], (tm, RNN_BLK))
        for k in range(CONV_W):
            s = CONV_W - 1 - k
            if s == 0:
                term = xr
            else:
                if prompt:
                    hist = jnp.tile(pltpu.roll(tail_s[:, cols], s, 0), (groups, 1))
                else:
                    hist = pltpu.roll(st_ref[:, cols], tm - 8 + s, 0)
                term = jnp.where(tseq >= s, pltpu.roll(xr, s, 0), hist)
            xc = xc + term * wconv_ref[k:k + 1, cols]
        g = jnp.dot(xc.astype(bf16), wrg_ref[n], preferred_element_type=f32)
        r = jax.nn.sigmoid(g[:, :RNN_BLK] + brg_ref[0:1, cols])
        i = jax.nn.sigmoid(g[:, RNN_BLK:] + brg_ref[1:2, cols])
        log_a = c_ref[:, cols] * r
        a = jnp.exp(log_a)
        b = jnp.sqrt(-_expm1(2.0 * log_a)) * (i * xc)
        for s in (1, 2, 4):
            m = t8 >= s
            a_sh = pltpu.roll(a, s, 0)
            b_sh = pltpu.roll(b, s, 0)
            b = jnp.where(m, a * b_sh + b, b)
            a = jnp.where(m, a * a_sh, a)
        a_s[:, cols] = a
        b_s[:, cols] = b

    if prompt:
        tail_s[...] = xr_ref[tm - 8:tm, :]

        def body(c, h):
            r0 = pl.multiple_of(c * 8, 8)
            hg = a_s[pl.ds(r0, 8), :] * h + b_s[pl.ds(r0, 8), :]
            b_s[pl.ds(r0, 8), :] = hg
            return jnp.broadcast_to(hg[7:8, :], (8, D_RNN))

        h = lax.fori_loop(0, groups, body, hc_s[...])
        hc_s[...] = h
        hout_ref[...] = h
    else:
        b_s[...] = a_s[...] * h0_ref[...] + b_s[...]
        hout_ref[...] = b_s[...]

    gr = gr_ref[...]
    u = (b_s[...] * (gr * jax.nn.sigmoid(gr))).astype(bf16)
    y_ref[...] = jnp.dot(u, wproj_ref[...], preferred_element_type=f32)


def _rglru_prompt_kernel(xr_ref, gr_ref, wconv_ref, bconv_ref, wrg_ref, brg_ref, c_ref, wproj_ref,
                         y_ref, hout_ref, a_s, b_s, tail_s, hc_s):
    _rglru_core(True, xr_ref, gr_ref, None, None, wconv_ref, bconv_ref, wrg_ref, brg_ref, c_ref,
                wproj_ref, y_ref, hout_ref, a_s, b_s, tail_s, hc_s)


def _rglru_sample_kernel(xr_ref, gr_ref, st_ref, h0_ref, wconv_ref, bconv_ref, wrg_ref, brg_ref, c_ref,
                         wproj_ref, y_ref, hout_ref, a_s, b_s):
    _rglru_core(False, xr_ref, gr_ref, st_ref, h0_ref, wconv_ref, bconv_ref, wrg_ref, brg_ref, c_ref,
                wproj_ref, y_ref, hout_ref, a_s, b_s, None, None)


def _const_spec(shape):
    nd = len(shape)
    return pl.BlockSpec(shape, lambda *_: (0,) * nd)


def _rglru_weights(w_conv, b_conv, w_rg, b_rg, lam, w_rnn_proj):
    wrg = jnp.concatenate([w_rg[0], w_rg[1]], axis=-1).astype(bf16)
    c = (-C_SCALE * jax.nn.softplus(-lam.astype(f32))).reshape(1, D_RNN)
    return (w_conv, b_conv.reshape(1, D_RNN), wrg, b_rg, c, w_rnn_proj.astype(bf16))


def _rglru_wspecs():
    return [_const_spec((CONV_W, D_RNN)), _const_spec((1, D_RNN)), _const_spec((RNN_BLOCKS, RNN_BLK, 2 * RNN_BLK)),
            _const_spec((2, D_RNN)), _const_spec((1, D_RNN)), _const_spec((D_RNN, D_MODEL))]


def _rglru_prompt(z, weights):
    T = z.shape[0]
    tm = 256
    return pl.pallas_call(
        _rglru_prompt_kernel,
        grid=(T // tm,),
        in_specs=[pl.BlockSpec((tm, D_RNN), lambda i: (i, COL_XR // D_RNN)),
                  pl.BlockSpec((tm, D_RNN), lambda i: (i, COL_GR // D_RNN))] + _rglru_wspecs(),
        out_specs=[pl.BlockSpec((tm, D_MODEL), lambda i: (i, 0)), _const_spec((8, D_RNN))],
        out_shape=[jax.ShapeDtypeStruct((T, D_MODEL), f32), jax.ShapeDtypeStruct((8, D_RNN), f32)],
        scratch_shapes=[pltpu.VMEM((tm, D_RNN), f32), pltpu.VMEM((tm, D_RNN), f32),
                        pltpu.VMEM((8, D_RNN), f32), pltpu.VMEM((8, D_RNN), f32)],
        compiler_params=_params(("arbitrary",), 56),
        name="rglru_prompt",
    )(z, z, *weights)


def _rglru_sample(z, stpad, h0x, weights):
    T = z.shape[0]
    return pl.pallas_call(
        _rglru_sample_kernel,
        grid=(1,),
        in_specs=[pl.BlockSpec((T, D_RNN), lambda i: (0, COL_XR // D_RNN)),
                  pl.BlockSpec((T, D_RNN), lambda i: (0, COL_GR // D_RNN)),
                  _const_spec((T, D_RNN)), _const_spec((T, D_RNN))] + _rglru_wspecs(),
        out_specs=[_const_spec((T, D_MODEL)), _const_spec((T, D_RNN))],
        out_shape=[jax.ShapeDtypeStruct((T, D_MODEL), f32), jax.ShapeDtypeStruct((T, D_RNN), f32)],
        scratch_shapes=[pltpu.VMEM((T, D_RNN), f32), pltpu.VMEM((T, D_RNN), f32)],
        compiler_params=_params(("arbitrary",), 56),
        name="rglru_sample",
    )(z, z, stpad, h0x, *weights)


def _compress_mlp(load_rows, pe_ref, w1_ref, w2_ref, e):
    acc = None
    for lp in range(L_CMP // 2):
        parts = []
        for dl in range(2):
            l = 2 * lp + dl
            parts.append((load_rows(l) + pe_ref[e, l:l + 1, :]).astype(bf16))
        d = jnp.dot(jnp.concatenate(parts, axis=1), w1_ref[e, lp], preferred_element_type=f32)
        acc = d if acc is None else acc + d
    hid = acc * jax.nn.sigmoid(acc)
    return jnp.dot(hid.astype(bf16), w2_ref[e], preferred_element_type=f32)


def _compress_prompt_kernel(x_ref, pe_ref, w1_ref, w2_ref, o_ref):
    e = pl.program_id(0)
    nb = o_ref.shape[3]

    def load_rows(l):
        return jnp.concatenate([x_ref[pl.ds(l, nb, stride=2 * L_CMP), :],
                                x_ref[pl.ds(L_CMP + l, nb, stride=2 * L_CMP), :]], axis=0)

    res = _compress_mlp(load_rows, pe_ref, w1_ref, w2_ref, e)
    o_ref[0, 0, 0] = res[:nb]
    o_ref[0, 0, 1] = res[nb:]


def _cmp_weights(w_cmp1, w_cmp2, pe_cmp):
    w1 = w_cmp1.reshape(2, L_CMP // 2, 2 * HEAD_DIM, HEAD_DIM).astype(bf16)
    return pe_cmp, w1, w_cmp2.astype(bf16)


def _cmp_wspecs():
    return [_const_spec((2, L_CMP, HEAD_DIM)), _const_spec((2, L_CMP // 2, 2 * HEAD_DIM, HEAD_DIM)),
            _const_spec((2, HEAD_DIM, HEAD_DIM))]


def _compress_prompt(z, cw):
    T = z.shape[0]
    nb = T // (2 * L_CMP)
    col0 = COL_KV // HEAD_DIM
    return pl.pallas_call(
        _compress_prompt_kernel,
        grid=(2, N_KV),
        in_specs=[pl.BlockSpec((T, HEAD_DIM), lambda e, g: (0, col0 + e * N_KV + g))] + _cmp_wspecs(),
        out_specs=pl.BlockSpec((1, 1, 2, nb, HEAD_DIM), lambda e, g: (e, g, 0, 0, 0)),
        out_shape=jax.ShapeDtypeStruct((2, N_KV, 2, nb, HEAD_DIM), f32),
        compiler_params=_params(("arbitrary", "arbitrary"), 32),
        name="compress_prompt",
    )(z, *cw)


CMP_PAGES = 16


def _compress_sample_kernel(pt_ref, cache_ref, pe8_ref, w1_ref, w2_ref, o_ref, buf, sem):
    b = pl.program_id(0)
    c = pl.program_id(1)
    nchunk = pl.num_programs(1)
    step = b * nchunk + c
    total = pl.num_programs(0) * nchunk
    slot = step % 2
    nblk = CMP_PAGES * PAGE_SIZE // L_CMP
    rows = nblk * 2 * N_KV

    def copies(bb, cc, sl):
        return [pltpu.make_async_copy(cache_ref.at[pt_ref[bb, cc * CMP_PAGES + p]],
                                      buf.at[sl, pl.ds(p * PAGE_SIZE, PAGE_SIZE)], sem.at[sl])
                for p in range(CMP_PAGES)]

    @pl.when(step == 0)
    def _():
        for cp in copies(b, c, slot):
            cp.start()

    @pl.when(step + 1 < total)
    def _():
        nxt = step + 1
        for cp in copies(nxt // nchunk, nxt % nchunk, 1 - slot):
            cp.start()

    for cp in copies(b, c, slot):
        cp.wait()

    is_k = lax.broadcasted_iota(jnp.int32, (rows, HEAD_DIM), 0) % (2 * N_KV) < N_KV
    acc = None
    for lp in range(L_CMP // 2):
        parts = []
        for dl in range(2):
            l = 2 * lp + dl
            x = buf[slot, pl.ds(l, nblk, stride=L_CMP), :, :] + pe8_ref[l]
            parts.append(x.reshape(rows, HEAD_DIM).astype(bf16))
        d = jnp.dot(jnp.concatenate(parts, axis=1), w1_ref[lp], preferred_element_type=f32)
        acc = d if acc is None else acc + d
    hid = jnp.where(is_k, acc[:, :HEAD_DIM], acc[:, HEAD_DIM:])
    hid = hid * jax.nn.sigmoid(hid)
    res = jnp.dot(hid.astype(bf16), w2_ref[...], preferred_element_type=f32)
    res = jnp.where(is_k, res[:, :HEAD_DIM], res[:, HEAD_DIM:])
    o_ref[0] = res.reshape(nblk, 2 * N_KV, HEAD_DIM)


def _compress_sample(page_table, cache, cw):
    pe_cmp, w1, w2 = cw
    Bs, n_pages = page_table.shape
    n_pool = cache.shape[0]
    nchunk = n_pages // CMP_PAGES
    nblk = CMP_PAGES * PAGE_SIZE // L_CMP
    eg = 2 * N_KV
    pe8 = jnp.repeat(jnp.transpose(pe_cmp, (1, 0, 2)), N_KV, axis=1)
    w1cat = jnp.concatenate([w1[0], w1[1]], axis=-1)
    w2cat = jnp.concatenate([w2[0], w2[1]], axis=-1)
    grid_spec = pltpu.PrefetchScalarGridSpec(
        num_scalar_prefetch=1,
        grid=(Bs, nchunk),
        in_specs=[pl.BlockSpec(memory_space=pl.ANY),
                  pl.BlockSpec(pe8.shape, lambda b, c, pt: (0, 0, 0)),
                  pl.BlockSpec(w1cat.shape, lambda b, c, pt: (0, 0, 0)),
                  pl.BlockSpec(w2cat.shape, lambda b, c, pt: (0, 0))],
        out_specs=pl.BlockSpec((1, nblk, eg, HEAD_DIM), lambda b, c, pt: (b, c, 0, 0)),
        scratch_shapes=[pltpu.VMEM((2, CMP_PAGES * PAGE_SIZE, eg, HEAD_DIM), f32), pltpu.SemaphoreType.DMA((2,))],
    )
    return pl.pallas_call(
        _compress_sample_kernel,
        grid_spec=grid_spec,
        out_shape=jax.ShapeDtypeStruct((Bs, nchunk * nblk, eg, HEAD_DIM), f32),
        compiler_params=_params(("arbitrary", "arbitrary"), 40),
        name="compress_sample",
    )(page_table, cache.reshape(n_pool, PAGE_SIZE, eg, HEAD_DIM), pe8, w1cat, w2cat)


def _cmp_attn_kernel(pos_base, pos_step, n_blk, as_bias, per_seq, q_ref, ckv_ref, o_ref, sel_ref, ck_s, sem):
    i = pl.program_id(0)
    tq = q_ref.shape[0]
    nb = ckv_ref.shape[1]
    nc = 2 * nb

    def copies(seq, sl):
        return [pltpu.make_async_copy(ckv_ref.at[seq, :, par, eg, :], ck_s.at[sl, eg, pl.ds(par * nb, nb), :],
                                      sem.at[sl])
                for eg in range(2 * N_KV) for par in range(2)]

    if per_seq:
        slot = i % 2

        @pl.when(i == 0)
        def _():
            for cp in copies(i, slot):
                cp.start()

        @pl.when(i + 1 < pl.num_programs(0))
        def _():
            for cp in copies(i + 1, 1 - slot):
                cp.start()

        for cp in copies(i, slot):
            cp.wait()
    else:
        slot = 0

        @pl.when(i == 0)
        def _():
            for cp in copies(0, 0):
                cp.start()
            for cp in copies(0, 0):
                cp.wait()

    width = sel_ref.shape[-1] if as_bias else pl.cdiv(n_blk, HEAD_DIM) * HEAD_DIM
    m_rows = GRP * tq
    pos0 = pos_base + i * pos_step
    pos_col = pos0 + lax.broadcasted_iota(jnp.int32, (tq, 1), 0)
    pos_rows = jnp.concatenate([pos_col] * GRP, axis=0)
    col = lax.broadcasted_iota(jnp.int32, (m_rows, nc), 1)
    c_idx = 2 * (col % nb) + col // nb
    readable = ((c_idx + 1) * L_CMP - 1) <= pos_rows

    lane = lax.broadcasted_iota(jnp.int32, (tq, width), 1).astype(f32)
    jt = (pos_col // L_SLC).astype(f32)
    forced = (lane == 0.0) | ((lane >= jt - (N_LOCAL - 1)) & (lane <= jt))
    future = lane > jt
    exists = lane < float(n_blk)
    out_lane = lax.broadcasted_iota(jnp.int32, (tq, HEAD_DIM), 1)

    for g in range(N_KV):
        ck = ck_s[slot, g].astype(bf16)
        cv = ck_s[slot, N_KV + g].astype(bf16)
        qg = jnp.concatenate([q_ref[:, (g * GRP + r) * HEAD_DIM:(g * GRP + r + 1) * HEAD_DIM]
                              for r in range(GRP)], axis=0).astype(bf16)
        s = lax.dot_general(qg, ck, (((1,), (1,)), ((), ())), preferred_element_type=f32) * SCALE
        s = jnp.where(readable, s, -jnp.inf)
        mx = jnp.max(s, axis=-1, keepdims=True)
        mx = jnp.where(mx > -jnp.inf, mx, 0.0)
        ex = jnp.where(readable, jnp.exp(s - mx), 0.0)
        p = ex / jnp.maximum(jnp.sum(ex, axis=-1, keepdims=True), 1e-30)
        o = jnp.dot(p.astype(bf16), cv, preferred_element_type=f32)
        for r in range(GRP):
            o_ref[:, (g * GRP + r) * HEAD_DIM:(g * GRP + r + 1) * HEAD_DIM] = o[r * tq:(r + 1) * tq]

        pr = p[0:tq]
        for r in range(1, GRP):
            pr = pr + p[r * tq:(r + 1) * tq]
        ps = pr[:, :nb] + pr[:, nb:]
        if width > nb:
            ps = jnp.concatenate([ps, jnp.zeros((tq, width - nb), f32)], axis=1)
        score = jnp.where(future, -jnp.inf, jnp.where(forced, FORCE, ps))
        removed = jnp.logical_not(exists)
        idx_out = jnp.zeros((tq, HEAD_DIM), jnp.int32)
        for k in range(N_SEL):
            se = jnp.where(removed, -jnp.inf, score)
            top = jnp.max(se, axis=-1, keepdims=True)
            cand = jnp.where((se == top) & jnp.logical_not(removed), lane, float(width))
            idx = jnp.min(cand, axis=-1, keepdims=True)
            removed = removed | (lane == idx)
            idx_out = jnp.where(out_lane == k, idx.astype(jnp.int32), idx_out)
        if as_bias:
            sel_ref[g] = jnp.where(removed & exists, 0.0, NEG).astype(bf16)
        else:
            sel_ref[:, g * HEAD_DIM:(g + 1) * HEAD_DIM] = idx_out


def _cmp_attn(z, ckv, tq, pos_base, pos_step, n_blk, as_bias):
    T = z.shape[0]
    n_seq, nc = ckv.shape[0], ckv.shape[1]
    nb = nc // 2
    per_seq = n_seq > 1
    assert n_seq == (T // tq if per_seq else 1)
    if as_bias:
        sel_spec = pl.BlockSpec((N_KV, tq, n_blk), lambda i: (0, i, 0))
        sel_shape = jax.ShapeDtypeStruct((N_KV, T, n_blk), bf16)
    else:
        sel_spec = pl.BlockSpec((tq, N_KV * HEAD_DIM), lambda i: (i, 0))
        sel_shape = jax.ShapeDtypeStruct((T, N_KV * HEAD_DIM), jnp.int32)
    kern = functools.partial(_cmp_attn_kernel, pos_base, pos_step, n_blk, as_bias, per_seq)
    return pl.pallas_call(
        kern,
        grid=(T // tq,),
        in_specs=[pl.BlockSpec((tq, N_HEADS * HEAD_DIM), lambda i: (i, COL_Q)),
                  pl.BlockSpec(memory_space=pl.ANY)],
        out_specs=[pl.BlockSpec((tq, N_HEADS * HEAD_DIM), lambda i: (i, 0)), sel_spec],
        out_shape=[jax.ShapeDtypeStruct((T, N_HEADS * HEAD_DIM), f32), sel_shape],
        scratch_shapes=[pltpu.VMEM((2 if per_seq else 1, 2 * N_KV, nc, HEAD_DIM), f32),
                        pltpu.SemaphoreType.DMA((2 if per_seq else 1,))],
        compiler_params=_params(("arbitrary",), 40),
        name="cmp_attn",
    )(z, ckv.reshape(n_seq, nb, 2, 2 * N_KV, HEAD_DIM))


FLASH_T = 256
SLC_TK = 1024
WIN_TK = WINDOW + FLASH_T
CAST_ROWS = 512
M_INIT = -3e38
LOG2E = 1.4426950408889634


def _flash_kernel(windowed, tk, q_ref, k_ref, v_ref, *rest):
    if windowed:
        o_ref, kb_s, vb_s, qa_s = rest
    else:
        bias_ref, oh_ref, o_ref, kb_s, vb_s, qa_s = rest
    tq = q_ref.shape[0]
    T = k_ref.shape[0]
    m_rows = GRP * tq
    qi = pl.program_id(1)

    @pl.when(qi == 0)
    def _():
        def cast(c, _):
            r0 = pl.multiple_of(c * CAST_ROWS, CAST_ROWS)
            kb_s[pl.ds(r0, CAST_ROWS), 0:HEAD_DIM] = k_ref[pl.ds(r0, CAST_ROWS), :].astype(bf16)
            if not windowed:
                kb_s[pl.ds(r0, CAST_ROWS), HEAD_DIM:2 * HEAD_DIM] = oh_ref[pl.ds(r0, CAST_ROWS), :]
            vb_s[pl.ds(r0, CAST_ROWS), :] = v_ref[pl.ds(r0, CAST_ROWS), :].astype(bf16)
            return 0
        lax.fori_loop(0, T // CAST_ROWS, cast, 0)

    for r in range(GRP):
        qa_s[r * tq:(r + 1) * tq, 0:HEAD_DIM] = (
            q_ref[:, r * HEAD_DIM:(r + 1) * HEAD_DIM] * (SCALE * LOG2E)).astype(bf16)
        if not windowed:
            qa_s[r * tq:(r + 1) * tq, HEAD_DIM:2 * HEAD_DIM] = bias_ref[0]
    def tile(k0, carry, masked):
        m, l, acc = carry
        s = lax.dot_general(qa_s[...], kb_s[pl.ds(k0, tk), :], (((1,), (1,)), ((), ())),
                            preferred_element_type=f32)
        if masked:
            t_loc = lax.broadcasted_iota(jnp.int32, (m_rows, 1), 0) % tq
            d = (qi * tq + t_loc - k0) - lax.broadcasted_iota(jnp.int32, (m_rows, tk), 1)
            ok = (d >= 0) & (d <= WINDOW) if windowed else d >= 0
            s = jnp.where(ok, s, NEG)
        m_new = jnp.maximum(m, jnp.max(s, axis=-1, keepdims=True))
        alpha = jnp.exp2(m - m_new)
        p = jnp.exp2(s - m_new)
        l = alpha * l + jnp.sum(p, axis=-1, keepdims=True)
        acc = alpha * acc + jnp.dot(p.astype(bf16), vb_s[pl.ds(k0, tk), :], preferred_element_type=f32)
        return m_new, l, acc

    carry = (jnp.full((m_rows, 1), M_INIT, f32), jnp.zeros((m_rows, 1), f32), jnp.zeros((m_rows, HEAD_DIM), f32))
    if windowed:
        carry = tile(pl.multiple_of(jnp.maximum(qi * tq - WINDOW, 0), tq), carry, True)
    else:
        n_full = (qi * tq) // tk
        carry = lax.fori_loop(0, n_full, lambda kt, c: tile(pl.multiple_of(kt * tk, tk), c, False), carry)
        carry = tile(pl.multiple_of(n_full * tk, tk), carry, True)
    _, l, acc = carry
    o = acc / l
    for r in range(GRP):
        o_ref[:, r * HEAD_DIM:(r + 1) * HEAD_DIM] = o[r * tq:(r + 1) * tq]


def _flash_prompt(z, branch, bias=None):
    T = z.shape[0]
    tq = FLASH_T
    windowed = bias is None
    tk = WIN_TK if windowed else SLC_TK
    m_rows = GRP * tq
    assert WINDOW % tq == 0 and T % SLC_TK == 0 and SLC_TK % tq == 0 and T >= WIN_TK
    kcol = COL_KV // HEAD_DIM + branch * 2 * N_KV
    in_specs = [pl.BlockSpec((tq, GRP * HEAD_DIM), lambda g, i: (i, g)),
                pl.BlockSpec((T, HEAD_DIM), lambda g, i: (0, kcol + g)),
                pl.BlockSpec((T, HEAD_DIM), lambda g, i: (0, kcol + N_KV + g))]
    args = [z, z, z]
    kw = HEAD_DIM
    if not windowed:
        n_blk = bias.shape[-1]
        onehot = (jnp.arange(T)[:, None] // L_SLC == jnp.arange(n_blk)[None, :]).astype(bf16)
        in_specs += [pl.BlockSpec((1, tq, n_blk), lambda g, i: (g, i, 0)), _const_spec((T, n_blk))]
        args += [bias, onehot]
        kw = HEAD_DIM + n_blk
    return pl.pallas_call(
        functools.partial(_flash_kernel, windowed, tk),
        grid=(N_KV, T // tq),
        in_specs=in_specs,
        out_specs=pl.BlockSpec((tq, GRP * HEAD_DIM), lambda g, i: (i, g)),
        out_shape=jax.ShapeDtypeStruct((T, N_HEADS * HEAD_DIM), f32),
        scratch_shapes=[pltpu.VMEM((T, kw), bf16), pltpu.VMEM((T, HEAD_DIM), bf16),
                        pltpu.VMEM((m_rows, kw), bf16)],
        compiler_params=_params(("arbitrary", "arbitrary"), 52),
        name="flash_win" if windowed else "flash_slc",
    )(*args)


def _slc_sample_kernel(nb_past, pos_base, idx_ref, pt_ref, q_ref, pool_ref, newb_ref, o_ref, kbuf, vbuf, sem):
    b = pl.program_id(0)
    g = pl.program_id(1)
    ts = q_ref.shape[0]
    n_pages = pt_ref.shape[1]
    nbn = newb_ref.shape[1] // L_SLC
    blocks_per_page = PAGE_SIZE // L_SLC
    n_copy = ts * N_SEL

    def block_id(c):
        t = c // N_SEL
        return idx_ref[((b * ts + t) * N_KV + g) * N_SEL + c % N_SEL]

    def issue(c, _):
        t = c // N_SEL
        n = c % N_SEL
        bid = block_id(c)
        dst_k = kbuf.at[t, pl.ds(pl.multiple_of(n * L_SLC, L_SLC), L_SLC), :]
        dst_v = vbuf.at[t, pl.ds(pl.multiple_of(n * L_SLC, L_SLC), L_SLC), :]

        @pl.when(bid < nb_past)
        def _():
            page = pt_ref[b, jnp.minimum(bid // blocks_per_page, n_pages - 1)]
            rows = pl.ds(pl.multiple_of((bid % blocks_per_page) * L_SLC, L_SLC), L_SLC)
            pltpu.make_async_copy(pool_ref.at[page, rows, g, :], dst_k, sem.at[0]).start()
            pltpu.make_async_copy(pool_ref.at[page, rows, N_KV + g, :], dst_v, sem.at[0]).start()

        @pl.when(bid >= nb_past)
        def _():
            nid = jnp.clip(bid - nb_past, 0, nbn - 1)
            rows = pl.ds(pl.multiple_of(nid * L_SLC, L_SLC), L_SLC)
            pltpu.make_async_copy(newb_ref.at[b, rows, g, :], dst_k, sem.at[0]).start()
            pltpu.make_async_copy(newb_ref.at[b, rows, N_KV + g, :], dst_v, sem.at[0]).start()
        return 0

    lax.fori_loop(0, n_copy, issue, 0)

    def drain(c, _):
        pltpu.make_async_copy(newb_ref.at[0, pl.ds(0, L_SLC), 0, :], kbuf.at[0, pl.ds(0, L_SLC), :], sem.at[0]).wait()
        pltpu.make_async_copy(newb_ref.at[0, pl.ds(0, L_SLC), 0, :], vbuf.at[0, pl.ds(0, L_SLC), :], sem.at[0]).wait()
        return 0

    lax.fori_loop(0, n_copy, drain, 0)

    lane = lax.broadcasted_iota(jnp.int32, (1, N_SEL * L_SLC), 1)
    for t in range(ts):
        pos = pos_base + t
        kpos = jnp.zeros((1, N_SEL * L_SLC), jnp.int32)
        for n in range(N_SEL):
            bid = block_id(t * N_SEL + n)
            kpos = jnp.where(lane // L_SLC == n, bid * L_SLC + lane % L_SLC, kpos)
        ok = kpos <= pos
        qt = jnp.concatenate([q_ref[t:t + 1, r * HEAD_DIM:(r + 1) * HEAD_DIM] for r in range(GRP)], axis=0)
        s = lax.dot_general(qt.astype(bf16), kbuf[t].astype(bf16), (((1,), (1,)), ((), ())),
                            preferred_element_type=f32) * SCALE
        s = jnp.where(ok, s, -jnp.inf)
        mx = jnp.max(s, axis=-1, keepdims=True)
        mx = jnp.where(mx > -jnp.inf, mx, 0.0)
        ex = jnp.where(ok, jnp.exp(s - mx), 0.0)
        p = ex / jnp.maximum(jnp.sum(ex, axis=-1, keepdims=True), 1e-30)
        o = jnp.dot(p.astype(bf16), vbuf[t].astype(bf16), preferred_element_type=f32)
        for r in range(GRP):
            o_ref[t:t + 1, r * HEAD_DIM:(r + 1) * HEAD_DIM] = o[r:r + 1, :]


def _slc_sample(zs, idx_flat, page_table, pool, newb, ts, pos_base):
    Bs = page_table.shape[0]
    n_pool = pool.shape[0]
    nb_past = page_table.shape[1] * (PAGE_SIZE // L_SLC)
    grid_spec = pltpu.PrefetchScalarGridSpec(
        num_scalar_prefetch=2,
        grid=(Bs, N_KV),
        in_specs=[pl.BlockSpec((ts, GRP * HEAD_DIM), lambda b, g, *_: (b, g)),
                  pl.BlockSpec(memory_space=pl.ANY), pl.BlockSpec(memory_space=pl.ANY)],
        out_specs=pl.BlockSpec((ts, GRP * HEAD_DIM), lambda b, g, *_: (b, g)),
        scratch_shapes=[pltpu.VMEM((ts, N_SEL * L_SLC, HEAD_DIM), f32), pltpu.VMEM((ts, N_SEL * L_SLC, HEAD_DIM), f32),
                        pltpu.SemaphoreType.DMA((1,))],
    )
    return pl.pallas_call(
        functools.partial(_slc_sample_kernel, nb_past, pos_base),
        grid_spec=grid_spec,
        out_shape=jax.ShapeDtypeStruct((Bs * ts, N_HEADS * HEAD_DIM), f32),
        compiler_params=_params(("arbitrary", "arbitrary"), 32),
        name="slc_sample",
    )(idx_flat, page_table, zs, pool.reshape(n_pool, PAGE_SIZE, 2 * N_KV, HEAD_DIM), newb)


def _win_sample_kernel(q_ref, new_ref, win_ref, o_ref, kall, sem):
    b = pl.program_id(0)
    ts = q_ref.shape[0]
    wb = win_ref.shape[1]
    rows = kall.shape[1]

    def copies():
        return [pltpu.make_async_copy(win_ref.at[b, :, eg, :], kall.at[eg, pl.ds(0, wb), :], sem.at[0])
                for eg in range(2 * N_KV)]

    for cp in copies():
        cp.start()
    for eg in range(2 * N_KV):
        kall[eg, wb:wb + ts, :] = new_ref[:, eg * HEAD_DIM:(eg + 1) * HEAD_DIM]
        kall[eg, wb + ts:rows, :] = jnp.zeros((rows - wb - ts, HEAD_DIM), f32)
    for cp in copies():
        cp.wait()

    m_rows = GRP * ts
    tq = lax.broadcasted_iota(jnp.int32, (m_rows, rows), 0) % ts
    ki = lax.broadcasted_iota(jnp.int32, (m_rows, rows), 1)
    d = tq + wb - ki
    ok = (d >= 0) & (d <= WINDOW) & (ki < wb + ts)
    for g in range(N_KV):
        qg = jnp.concatenate([q_ref[:, (g * GRP + r) * HEAD_DIM:(g * GRP + r + 1) * HEAD_DIM]
                              for r in range(GRP)], axis=0).astype(bf16)
        s = lax.dot_general(qg, kall[g].astype(bf16), (((1,), (1,)), ((), ())),
                            preferred_element_type=f32) * SCALE
        s = jnp.where(ok, s, -jnp.inf)
        mx = jnp.max(s, axis=-1, keepdims=True)
        mx = jnp.where(mx > -jnp.inf, mx, 0.0)
        ex = jnp.where(ok, jnp.exp(s - mx), 0.0)
        p = ex / jnp.maximum(jnp.sum(ex, axis=-1, keepdims=True), 1e-30)
        o = jnp.dot(p.astype(bf16), kall[N_KV + g].astype(bf16), preferred_element_type=f32)
        for r in range(GRP):
            o_ref[:, (g * GRP + r) * HEAD_DIM:(g * GRP + r + 1) * HEAD_DIM] = o[r * ts:(r + 1) * ts]


def _win_sample(zs, win_buf, ts):
    Bs, wb = win_buf.shape[0], win_buf.shape[1]
    rows = wb + HEAD_DIM
    ncol = 2 * KV_W
    return pl.pallas_call(
        _win_sample_kernel,
        grid=(Bs,),
        in_specs=[pl.BlockSpec((ts, N_HEADS * HEAD_DIM), lambda b: (b, COL_Q)),
                  pl.BlockSpec((ts, ncol), lambda b: (b, (COL_KV + 2 * ncol) // ncol)),
                  pl.BlockSpec(memory_space=pl.ANY)],
        out_specs=pl.BlockSpec((ts, N_HEADS * HEAD_DIM), lambda b: (b, 0)),
        out_shape=jax.ShapeDtypeStruct((Bs * ts, N_HEADS * HEAD_DIM), f32),
        scratch_shapes=[pltpu.VMEM((2 * N_KV, rows, HEAD_DIM), f32), pltpu.SemaphoreType.DMA((1,))],
        compiler_params=_params(("arbitrary",), 32),
        name="win_sample",
    )(zs, zs, win_buf.reshape(Bs, wb, 2 * N_KV, HEAD_DIM))


def _nsa_out_kernel(oc_ref, os_ref, ow_ref, gn_ref, bg_ref, w_ref, y_ref, u_s):
    bgs = jax.nn.sigmoid(bg_ref[...])
    gn = gn_ref[...]
    for h in range(N_HEADS):
        cols = slice(h * HEAD_DIM, (h + 1) * HEAD_DIM)
        o = (bgs[:, h:h + 1] * oc_ref[:, cols] + bgs[:, N_HEADS + h:N_HEADS + h + 1] * os_ref[:, cols]
             + bgs[:, 2 * N_HEADS + h:2 * N_HEADS + h + 1] * ow_ref[:, cols])
        gh = gn[:, cols]
        u_s[:, cols] = (o * (gh * jax.nn.sigmoid(gh))).astype(bf16)
    y_ref[...] = jnp.dot(u_s[...], w_ref[...], preferred_element_type=f32)


def _nsa_out(o_cmp, o_slc, o_win, z, w):
    T = z.shape[0]
    tm = min(T, 256)
    row = lambda i: (i, 0)
    return pl.pallas_call(
        _nsa_out_kernel,
        grid=(T // tm,),
        in_specs=[pl.BlockSpec((tm, D_MODEL), row)] * 3 + [
            pl.BlockSpec((tm, D_MODEL), lambda i: (i, COL_GN // D_MODEL)),
            pl.BlockSpec((tm, TN), lambda i: (i, COL_BG // TN)),
            _const_spec((D_MODEL, D_MODEL))],
        out_specs=pl.BlockSpec((tm, D_MODEL), row),
        out_shape=jax.ShapeDtypeStruct((T, D_MODEL), f32),
        scratch_shapes=[pltpu.VMEM((tm, D_MODEL), bf16)],
        compiler_params=_params(("arbitrary",), 48),
        name="nsa_out",
    )(o_cmp, o_slc, o_win, z, z, w)


def _merge_kernel(yr_ref, yn_ref, m0_ref, m1_ref, x_ref, w_ref, o_ref):
    u = jax.nn.sigmoid(m0_ref[...]) * yr_ref[...] + jax.nn.sigmoid(m1_ref[...]) * yn_ref[...]
    o_ref[...] = x_ref[...] + jnp.dot(u.astype(bf16), w_ref[...], preferred_element_type=f32)


def _merge(y_rnn, y_nsa, z, x, w):
    T = z.shape[0]
    tm = min(T, 256)
    row = lambda i: (i, 0)
    return pl.pallas_call(
        _merge_kernel,
        grid=(T // tm,),
        in_specs=[pl.BlockSpec((tm, D_MODEL), row)] * 2 + [
            pl.BlockSpec((tm, D_MODEL), lambda i: (i, COL_MG // D_MODEL)),
            pl.BlockSpec((tm, D_MODEL), lambda i: (i, COL_MG // D_MODEL + 1)),
            pl.BlockSpec((tm, D_MODEL), row), _const_spec((D_MODEL, D_MODEL))],
        out_specs=pl.BlockSpec((tm, D_MODEL), row),
        out_shape=jax.ShapeDtypeStruct((T, D_MODEL), f32),
        compiler_params=_params(("arbitrary",), 48),
        name="merge_out",
    )(y_rnn, y_nsa, z, z, x, w)


def _ple_kernel(x_ref, p_ref, wp_ref, wg_ref, gf_ref, o_ref):
    x = x_ref[...]
    emb = jnp.dot(p_ref[...].astype(bf16), wp_ref[...], preferred_element_type=f32)
    gate = jax.nn.sigmoid(jnp.dot(x.astype(bf16), wg_ref[...], preferred_element_type=f32))
    x = x + emb * gate
    ms = jnp.mean(x * x, axis=-1, keepdims=True)
    o_ref[...] = x * lax.rsqrt(ms + EPS) * gf_ref[...]


def _ple_norm(x1, p, w_ple, w_gate, g_final):
    T = x1.shape[0]
    tm = min(T, 256)
    row = lambda i: (i, 0)
    return pl.pallas_call(
        _ple_kernel,
        grid=(T // tm,),
        in_specs=[pl.BlockSpec((tm, D_MODEL), row), pl.BlockSpec((tm, PLE_DIM), row),
                  _const_spec((PLE_DIM, D_MODEL)), _const_spec((D_MODEL, D_MODEL)), _const_spec((1, D_MODEL))],
        out_specs=pl.BlockSpec((tm, D_MODEL), row),
        out_shape=jax.ShapeDtypeStruct((T, D_MODEL), f32),
        compiler_params=_params(("arbitrary",), 40),
        name="ple_norm",
    )(x1, p, w_ple, w_gate, g_final)


def kernel(x_prompt, x_sample, cache_cmp_kv, cache_slc_kv, state_win_kv, state_rnn_h, state_rnn_conv, page_table,
           p_prompt, p_sample, g_norm, w_in, w_conv, b_conv, w_rg, b_rg, lam, w_cmp1, w_cmp2, pe_cmp,
           w_rnn_proj, w_nsa_proj, w_out, w_ple, w_ple_gate, g_final):
    B, T, _ = x_prompt.shape
    Bs, Ts, _ = x_sample.shape
    past = page_table.shape[1] * PAGE_SIZE
    assert B == 1 and w_in.shape[0] == 1, "single prompt sequence, single layer"
    assert Ts == 8 and T % 1024 == 0 and T >= CONV_W - 1 and Ts >= CONV_W - 1 and Ts < L_CMP
    w_perm = _permute_w_in(w_in[0])
    rg_w = _rglru_weights(w_conv[0], b_conv[0], w_rg[0], b_rg[0], lam[0], w_rnn_proj[0])

    cos_p, sin_p = _rope_tables(jnp.arange(T))
    zp = _in_proj(x_prompt.reshape(T, D_MODEL), g_norm, w_perm, cos_p, sin_p)
    cos_s, sin_s = _rope_tables(jnp.tile(past + jnp.arange(Ts), Bs))
    zs = _in_proj(x_sample.reshape(Bs * Ts, D_MODEL), g_norm, w_perm, cos_s, sin_s)

    y_rnn_p, h_p = _rglru_prompt(zp, rg_w)
    stpad = jnp.pad(state_rnn_conv[0], ((0, 0), (8 - (CONV_W - 1), 0), (0, 0))).reshape(Bs * Ts, D_RNN)
    h0x = jnp.repeat(state_rnn_h[0], Ts, axis=0)
    y_rnn_s, hr_s = _rglru_sample(zs, stpad, h0x, rg_w)

    cw = _cmp_weights(w_cmp1[0], w_cmp2[0], pe_cmp[0])
    ckv_p = jnp.transpose(_compress_prompt(zp, cw), (3, 2, 0, 1, 4)).reshape(1, T // L_CMP, 2 * N_KV, HEAD_DIM)
    ckv_s = _compress_sample(page_table, cache_cmp_kv[0], cw)
    n_blk_p = -(-T // L_SLC)
    n_blk_s = -(-(past + Ts) // L_SLC)
    o_cmp_p, bias_p = _cmp_attn(zp, ckv_p, FLASH_T, 0, FLASH_T, n_blk_p, True)
    o_cmp_s, idx_s = _cmp_attn(zs, ckv_s, Ts, past, 0, n_blk_s, False)
    o_slc_p = _flash_prompt(zp, 1, bias_p)
    o_win_p = _flash_prompt(zp, 2)

    ncol = 2 * KV_W
    nb_past = past // L_SLC
    new_rows = (n_blk_s - nb_past) * L_SLC
    newb = jnp.pad(zs[:, COL_KV + ncol:COL_KV + 2 * ncol].reshape(Bs, Ts, 2 * N_KV, HEAD_DIM),
                   ((0, 0), (0, new_rows - Ts), (0, 0), (0, 0)))
    idx_flat = idx_s.reshape(Bs * Ts, N_KV, HEAD_DIM)[:, :, :N_SEL].reshape(-1)
    o_slc_s = _slc_sample(zs, idx_flat, page_table, cache_slc_kv[0], newb, Ts, past)
    o_win_s = _win_sample(zs, state_win_kv[0], Ts)

    w_nsa_b, w_out_b = w_nsa_proj[0].astype(bf16), w_out[0].astype(bf16)
    w_ple_b, w_gate_b = w_ple[0].astype(bf16), w_ple_gate[0].astype(bf16)
    gf = g_final.reshape(1, D_MODEL)

    def tail(z, x, p, y_rnn, o_cmp, o_slc, o_win):
        y_nsa = _nsa_out(o_cmp, o_slc, o_win, z, w_nsa_b)
        x1 = _merge(y_rnn, y_nsa, z, x, w_out_b)
        return _ple_norm(x1, p, w_ple_b, w_gate_b, gf)

    y_p = tail(zp, x_prompt.reshape(T, D_MODEL), p_prompt.reshape(T, PLE_DIM), y_rnn_p, o_cmp_p, o_slc_p, o_win_p)
    y_s = tail(zs, x_sample.reshape(Bs * Ts, D_MODEL), p_sample.reshape(Bs * Ts, PLE_DIM), y_rnn_s, o_cmp_s, o_slc_s,
               o_win_s)

    kv_shape = (2, N_KV, HEAD_DIM)
    zs3 = zs.reshape(Bs, Ts, D_Z)
    keep_p = min(WINDOW, T)
    win_s = jnp.concatenate([state_win_kv[0], zs3[:, :, COL_KV + 2 * ncol:COL_KV + 3 * ncol].reshape(Bs, Ts, *kv_shape)],
                            axis=1)
    keep_s = min(WINDOW, past + Ts)
    hist = CONV_W - 1
    return (y_p.reshape(1, T, D_MODEL),
            y_s.reshape(Bs, Ts, D_MODEL),
            zp[:, COL_KV:COL_KV + ncol].reshape(1, 1, T, *kv_shape),
            zs3[:, :, COL_KV:COL_KV + ncol].reshape(1, Bs, Ts, *kv_shape),
            zp[:, COL_KV + ncol:COL_KV + 2 * ncol].reshape(1, 1, T, *kv_shape),
            zs3[:, :, COL_KV + ncol:COL_KV + 2 * ncol].reshape(1, Bs, Ts, *kv_shape),
            zp[T - keep_p:, COL_KV + 2 * ncol:COL_KV + 3 * ncol].reshape(1, 1, keep_p, *kv_shape),
            win_s[None, :, win_s.shape[1] - keep_s:],
            h_p[7:8].reshape(1, 1, D_RNN),
            hr_s.reshape(Bs, Ts, D_RNN)[None, :, Ts - 1],
            zp[T - hist:, COL_XR:COL_XR + D_RNN].reshape(1, 1, hist, D_RNN),
            zs3[None, :, Ts - hist:, COL_XR:COL_XR + D_RNN])
```

```python
import functools
import math

import jax
import jax.numpy as jnp
from jax import lax
from jax.experimental import pallas as pl
from jax.experimental.pallas import tpu as pltpu

f32 = jnp.float32
bf16 = jnp.bfloat16

D_MODEL = 2048
D_RNN = 2048
RNN_BLOCKS = 8
RNN_BLK = D_RNN // RNN_BLOCKS
CONV_W = 4
C_SCALE = 8.0
N_HEADS = 16
HEAD_DIM = 128
N_KV = 4
GRP = N_HEADS // N_KV
KV_W = N_KV * HEAD_DIM
L_CMP = 32
L_SLC = 64
PAGE_SIZE = 128
N_SEL = 16
N_LOCAL = 2
WINDOW = 512
FORCE = 1e4
SCALE = HEAD_DIM ** -0.5
ROPE_THETA = 10000.0
PLE_DIM = 256
EPS = 1e-6
NEG = -1e30

COL_Q = 0
COL_XR = 2048
COL_GR = 4096
COL_GN = 6144
COL_MG = 8192
COL_KV = 12288
COL_BG = 15360
D_Z = 15872
TN = 512
MIB = 1024 * 1024


def _params(sem, vmem_mib):
    return pltpu.CompilerParams(dimension_semantics=sem, vmem_limit_bytes=vmem_mib * MIB)


J_XR, J_GN, J_MG, J_KV, J_BG = COL_XR // TN, COL_GN // TN, COL_MG // TN, COL_KV // TN, COL_BG // TN
SRC_XR, SRC_Q, SRC_KV, SRC_GN, SRC_TAIL = 0, 4096 // TN, 6144 // TN, 9216 // TN, 11264


def _from_tail(j):
    return ((j >= J_MG) & (j < J_KV)) | (j >= J_BG)


def _w_src_block(j):
    return jnp.where(j < J_XR, j + SRC_Q,
                     jnp.where(j < J_GN, j - J_XR + SRC_XR,
                               jnp.where(j < J_MG, j - J_GN + SRC_GN,
                                         jnp.where(j < J_KV, J_MG - 1 - J_GN + SRC_GN,
                                                   jnp.where(j < J_BG, j - J_KV + SRC_KV, J_BG - 1 - J_KV + SRC_KV)))))


def _w_tail_block(j):
    n_mg = J_KV - J_MG
    return jnp.where(j < J_MG, 0, jnp.where(j < J_KV, j - J_MG, jnp.where(j < J_BG, n_mg - 1, n_mg)))


def _proj_kernel(x_ref, g_ref, w_ref, wt_ref, cos_ref, sin_ref, o_ref, hn_ref):
    j = pl.program_id(1)

    @pl.when(j == 0)
    def _():
        x = x_ref[...]
        ms = jnp.mean(x * x, axis=-1, keepdims=True)
        hn_ref[...] = (x * lax.rsqrt(ms + EPS) * g_ref[...]).astype(bf16)

    tail = _from_tail(j)

    @pl.when(jnp.logical_not(tail))
    def _():
        o_ref[...] = jnp.dot(hn_ref[...], w_ref[...].astype(bf16), preferred_element_type=f32)

    @pl.when(tail)
    def _():
        o_ref[...] = jnp.dot(hn_ref[...], wt_ref[...], preferred_element_type=f32)

    @pl.when((j < J_XR) | ((j >= J_KV) & (j < J_BG) & ((j - J_KV) % 2 == 0)))
    def _():
        c = cos_ref[...]
        s = sin_ref[...]
        for h in range(TN // HEAD_DIM):
            a = o_ref[:, h * HEAD_DIM:(h + 1) * HEAD_DIM]
            o_ref[:, h * HEAD_DIM:(h + 1) * HEAD_DIM] = a * c + pltpu.roll(a, HEAD_DIM // 2, 1) * s


def _in_proj(x, g_norm, w_in, w_tail, cos_t, sin_t):
    T = x.shape[0]
    tm = min(T, 1024)
    return pl.pallas_call(
        _proj_kernel,
        grid=(T // tm, D_Z // TN),
        in_specs=[
            pl.BlockSpec((tm, D_MODEL), lambda i, j: (i, 0)),
            pl.BlockSpec((1, D_MODEL), lambda i, j: (0, 0)),
            pl.BlockSpec((D_MODEL, TN), lambda i, j: (0, _w_src_block(j))),
            pl.BlockSpec((D_MODEL, TN), lambda i, j: (0, _w_tail_block(j))),
            pl.BlockSpec((tm, HEAD_DIM), lambda i, j: (i, 0)),
            pl.BlockSpec((tm, HEAD_DIM), lambda i, j: (i, 0)),
        ],
        out_specs=pl.BlockSpec((tm, TN), lambda i, j: (i, j)),
        out_shape=jax.ShapeDtypeStruct((T, D_Z), f32),
        scratch_shapes=[pltpu.VMEM((tm, D_MODEL), bf16)],
        compiler_params=_params(("arbitrary", "arbitrary"), 52),
        name="in_proj",
    )(x, g_norm, w_in, w_tail, cos_t, sin_t)


def _rope_tables(pos):
    half = HEAD_DIM // 2
    inv = ROPE_THETA ** (-jnp.arange(half, dtype=f32) / half)
    ang = pos.astype(f32)[:, None] * inv[None, :]
    cos = jnp.cos(ang)
    sin = jnp.sin(ang)
    return jnp.concatenate([cos, cos], axis=1), jnp.concatenate([-sin, sin], axis=1)


def _w_in_tail(w):
    n_bg = 3 * N_HEADS
    pad = jnp.zeros((D_MODEL, TN - n_bg), w.dtype)
    return jnp.concatenate([w[:, SRC_TAIL + n_bg:], w[:, SRC_TAIL:SRC_TAIL + n_bg], pad], axis=1).astype(bf16)


def _expm1(x):
    u = jnp.exp(x)
    near = jnp.where(u == 1.0, x, (u - 1.0) * x / jnp.log(u))
    return jnp.where(x < -0.5, u - 1.0, near)


def _rglru_core(prompt, xr_ref, gr_ref, st_ref, h0_ref, wconv_ref, bconv_ref, wrg_ref, brg_ref, c_ref,
                wproj_ref, y_ref, hout_ref, a_s, b_s, tail_s, hc_s):
    tm = xr_ref.shape[0]
    groups = tm // 8
    if prompt:
        @pl.when(pl.program_id(0) == 0)
        def _():
            tail_s[...] = jnp.zeros_like(tail_s)
            hc_s[...] = jnp.zeros_like(hc_s)

    row = lax.broadcasted_iota(jnp.int32, (tm, RNN_BLK), 0)
    t8 = row % 8
    tseq = row if prompt else t8
    for n in range(RNN_BLOCKS):
        cols = slice(n * RNN_BLK, (n + 1) * RNN_BLK)
        xr = xr_ref[:, cols]
        xc = jnp.broadcast_to(bconv_ref[:, cols], (tm, RNN_BLK))
        for k in range(CONV_W):
            s = CONV_W - 1 - k
            if s == 0:
                term = xr
            else:
                if prompt:
                    hist = jnp.tile(pltpu.roll(tail_s[:, cols], s, 0), (groups, 1))
                else:
                    hist = pltpu.roll(st_ref[:, cols], tm - 8 + s, 0)
                term = jnp.where(tseq >= s, pltpu.roll(xr, s, 0), hist)
            xc = xc + term * wconv_ref[k:k + 1, cols]
        g = jnp.dot(xc.astype(bf16), wrg_ref[n], preferred_element_type=f32)
        r = jax.nn.sigmoid(g[:, :RNN_BLK] + brg_ref[0:1, cols])
        i = jax.nn.sigmoid(g[:, RNN_BLK:] + brg_ref[1:2, cols])
        log_a = c_ref[:, cols] * r
        a = jnp.exp(log_a)
        b = jnp.sqrt(-_expm1(2.0 * log_a)) * (i * xc)
        for s in (1, 2, 4):
            m = t8 >= s
            a_sh = pltpu.roll(a, s, 0)
            b_sh = pltpu.roll(b, s, 0)
            b = jnp.where(m, a * b_sh + b, b)
            a = jnp.where(m, a * a_sh, a)
        a_s[:, cols] = a
        b_s[:, cols] = b

    if prompt:
        tail_s[...] = xr_ref[tm - 8:tm, :]

        def body(c, h):
            r0 = pl.multiple_of(c * 8, 8)
            hg = a_s[pl.ds(r0, 8), :] * h + b_s[pl.ds(r0, 8), :]
            b_s[pl.ds(r0, 8), :] = hg
            return jnp.broadcast_to(hg[7:8, :], (8, D_RNN))

        h = lax.fori_loop(0, groups, body, hc_s[...])
        hc_s[...] = h
        hout_ref[...] = h
    else:
        b_s[...] = a_s[...] * h0_ref[...] + b_s[...]
        hout_ref[...] = b_s[...]

    gr = gr_ref[...]
    u = (b_s[...] * (gr * jax.nn.sigmoid(gr))).astype(bf16)
    y_ref[...] = jnp.dot(u, wproj_ref[...], preferred_element_type=f32)


def _rglru_prompt_kernel(xr_ref, gr_ref, wconv_ref, bconv_ref, wrg_ref, brg_ref, c_ref, wproj_ref,
                         y_ref, hout_ref, a_s, b_s, tail_s, hc_s):
    _rglru_core(True, xr_ref, gr_ref, None, None, wconv_ref, bconv_ref, wrg_ref, brg_ref, c_ref,
                wproj_ref, y_ref, hout_ref, a_s, b_s, tail_s, hc_s)


def _rglru_sample_kernel(xr_ref, gr_ref, st_ref, h0_ref, wconv_ref, bconv_ref, wrg_ref, brg_ref, c_ref,
                         wproj_ref, y_ref, hout_ref, a_s, b_s):
    _rglru_core(False, xr_ref, gr_ref, st_ref, h0_ref, wconv_ref, bconv_ref, wrg_ref, brg_ref, c_ref,
                wproj_ref, y_ref, hout_ref, a_s, b_s, None, None)


def _const_spec(shape):
    nd = len(shape)
    return pl.BlockSpec(shape, lambda *_: (0,) * nd)


def _rglru_weights(w_conv, b_conv, w_rg, b_rg, lam, w_rnn_proj):
    wrg = jnp.concatenate([w_rg[0], w_rg[1]], axis=-1).astype(bf16)
    c = (-C_SCALE * jax.nn.softplus(-lam.astype(f32))).reshape(1, D_RNN)
    return (w_conv, b_conv.reshape(1, D_RNN), wrg, b_rg, c, w_rnn_proj.astype(bf16))


def _rglru_wspecs():
    return [_const_spec((CONV_W, D_RNN)), _const_spec((1, D_RNN)), _const_spec((RNN_BLOCKS, RNN_BLK, 2 * RNN_BLK)),
            _const_spec((2, D_RNN)), _const_spec((1, D_RNN)), _const_spec((D_RNN, D_MODEL))]


def _rglru_prompt(z, weights):
    T = z.shape[0]
    tm = 256
    return pl.pallas_call(
        _rglru_prompt_kernel,
        grid=(T // tm,),
        in_specs=[pl.BlockSpec((tm, D_RNN), lambda i: (i, COL_XR // D_RNN)),
                  pl.BlockSpec((tm, D_RNN), lambda i: (i, COL_GR // D_RNN))] + _rglru_wspecs(),
        out_specs=[pl.BlockSpec((tm, D_MODEL), lambda i: (i, 0)), _const_spec((8, D_RNN))],
        out_shape=[jax.ShapeDtypeStruct((T, D_MODEL), f32), jax.ShapeDtypeStruct((8, D_RNN), f32)],
        scratch_shapes=[pltpu.VMEM((tm, D_RNN), f32), pltpu.VMEM((tm, D_RNN), f32),
                        pltpu.VMEM((8, D_RNN), f32), pltpu.VMEM((8, D_RNN), f32)],
        compiler_params=_params(("arbitrary",), 56),
        name="rglru_prompt",
    )(z, z, *weights)


def _rglru_sample(z, stpad, h0x, weights):
    T = z.shape[0]
    return pl.pallas_call(
        _rglru_sample_kernel,
        grid=(1,),
        in_specs=[pl.BlockSpec((T, D_RNN), lambda i: (0, COL_XR // D_RNN)),
                  pl.BlockSpec((T, D_RNN), lambda i: (0, COL_GR // D_RNN)),
                  _const_spec((T, D_RNN)), _const_spec((T, D_RNN))] + _rglru_wspecs(),
        out_specs=[_const_spec((T, D_MODEL)), _const_spec((T, D_RNN))],
        out_shape=[jax.ShapeDtypeStruct((T, D_MODEL), f32), jax.ShapeDtypeStruct((T, D_RNN), f32)],
        scratch_shapes=[pltpu.VMEM((T, D_RNN), f32), pltpu.VMEM((T, D_RNN), f32)],
        compiler_params=_params(("arbitrary",), 56),
        name="rglru_sample",
    )(z, z, stpad, h0x, *weights)


def _compress_mlp(load_rows, pe_ref, w1_ref, w2_ref, e):
    acc = None
    for lp in range(L_CMP // 2):
        parts = []
        for dl in range(2):
            l = 2 * lp + dl
            parts.append((load_rows(l) + pe_ref[e, l:l + 1, :]).astype(bf16))
        d = jnp.dot(jnp.concatenate(parts, axis=1), w1_ref[e, lp], preferred_element_type=f32)
        acc = d if acc is None else acc + d
    hid = acc * jax.nn.sigmoid(acc)
    return jnp.dot(hid.astype(bf16), w2_ref[e], preferred_element_type=f32)


def _compress_prompt_kernel(x_ref, pe_ref, w1_ref, w2_ref, o_ref):
    e = pl.program_id(0)
    nb = o_ref.shape[3]

    def load_rows(l):
        return jnp.concatenate([x_ref[pl.ds(l, nb, stride=2 * L_CMP), :],
                                x_ref[pl.ds(L_CMP + l, nb, stride=2 * L_CMP), :]], axis=0)

    res = _compress_mlp(load_rows, pe_ref, w1_ref, w2_ref, e)
    o_ref[0, 0, 0] = res[:nb]
    o_ref[0, 0, 1] = res[nb:]


def _cmp_weights(w_cmp1, w_cmp2, pe_cmp):
    w1 = w_cmp1.reshape(2, L_CMP // 2, 2 * HEAD_DIM, HEAD_DIM).astype(bf16)
    return pe_cmp, w1, w_cmp2.astype(bf16)


def _cmp_wspecs():
    return [_const_spec((2, L_CMP, HEAD_DIM)), _const_spec((2, L_CMP // 2, 2 * HEAD_DIM, HEAD_DIM)),
            _const_spec((2, HEAD_DIM, HEAD_DIM))]


def _compress_prompt(z, cw):
    T = z.shape[0]
    nb = T // (2 * L_CMP)
    col0 = COL_KV // HEAD_DIM
    return pl.pallas_call(
        _compress_prompt_kernel,
        grid=(2, N_KV),
        in_specs=[pl.BlockSpec((T, HEAD_DIM), lambda e, g: (0, col0 + e * N_KV + g))] + _cmp_wspecs(),
        out_specs=pl.BlockSpec((1, 1, 2, nb, HEAD_DIM), lambda e, g: (e, g, 0, 0, 0)),
        out_shape=jax.ShapeDtypeStruct((2, N_KV, 2, nb, HEAD_DIM), f32),
        compiler_params=_params(("arbitrary", "arbitrary"), 32),
        name="compress_prompt",
    )(z, *cw)


CMP_PAGES = 16


def _compress_sample_kernel(pt_ref, cache_ref, pe8_ref, w1_ref, w2_ref, o_ref, buf, sem):
    b = pl.program_id(0)
    c = pl.program_id(1)
    nchunk = pl.num_programs(1)
    step = b * nchunk + c
    total = pl.num_programs(0) * nchunk
    slot = step % 2
    nblk = CMP_PAGES * PAGE_SIZE // L_CMP
    rows = nblk * 2 * N_KV

    def copies(bb, cc, sl):
        return [pltpu.make_async_copy(cache_ref.at[pt_ref[bb, cc * CMP_PAGES + p]],
                                      buf.at[sl, pl.ds(p * PAGE_SIZE, PAGE_SIZE)], sem.at[sl])
                for p in range(CMP_PAGES)]

    @pl.when(step == 0)
    def _():
        for cp in copies(b, c, slot):
            cp.start()

    @pl.when(step + 1 < total)
    def _():
        nxt = step + 1
        for cp in copies(nxt // nchunk, nxt % nchunk, 1 - slot):
            cp.start()

    for cp in copies(b, c, slot):
        cp.wait()

    is_k = lax.broadcasted_iota(jnp.int32, (rows, HEAD_DIM), 0) % (2 * N_KV) < N_KV
    acc = None
    for lp in range(L_CMP // 2):
        parts = []
        for dl in range(2):
            l = 2 * lp + dl
            x = buf[slot, pl.ds(l, nblk, stride=L_CMP), :, :] + pe8_ref[l]
            parts.append(x.reshape(rows, HEAD_DIM).astype(bf16))
        d = jnp.dot(jnp.concatenate(parts, axis=1), w1_ref[lp], preferred_element_type=f32)
        acc = d if acc is None else acc + d
    hid = jnp.where(is_k, acc[:, :HEAD_DIM], acc[:, HEAD_DIM:])
    hid = hid * jax.nn.sigmoid(hid)
    res = jnp.dot(hid.astype(bf16), w2_ref[...], preferred_element_type=f32)
    res = jnp.where(is_k, res[:, :HEAD_DIM], res[:, HEAD_DIM:])
    o_ref[0] = res.reshape(nblk, 2 * N_KV, HEAD_DIM)


def _compress_sample(page_table, cache, cw):
    pe_cmp, w1, w2 = cw
    Bs, n_pages = page_table.shape
    n_pool = cache.shape[0]
    nchunk = n_pages // CMP_PAGES
    nblk = CMP_PAGES * PAGE_SIZE // L_CMP
    eg = 2 * N_KV
    pe8 = jnp.repeat(jnp.transpose(pe_cmp, (1, 0, 2)), N_KV, axis=1)
    w1cat = jnp.concatenate([w1[0], w1[1]], axis=-1)
    w2cat = jnp.concatenate([w2[0], w2[1]], axis=-1)
    grid_spec = pltpu.PrefetchScalarGridSpec(
        num_scalar_prefetch=1,
        grid=(Bs, nchunk),
        in_specs=[pl.BlockSpec(memory_space=pl.ANY),
                  pl.BlockSpec(pe8.shape, lambda b, c, pt: (0, 0, 0)),
                  pl.BlockSpec(w1cat.shape, lambda b, c, pt: (0, 0, 0)),
                  pl.BlockSpec(w2cat.shape, lambda b, c, pt: (0, 0))],
        out_specs=pl.BlockSpec((1, nblk, eg, HEAD_DIM), lambda b, c, pt: (b, c, 0, 0)),
        scratch_shapes=[pltpu.VMEM((2, CMP_PAGES * PAGE_SIZE, eg, HEAD_DIM), f32), pltpu.SemaphoreType.DMA((2,))],
    )
    return pl.pallas_call(
        _compress_sample_kernel,
        grid_spec=grid_spec,
        out_shape=jax.ShapeDtypeStruct((Bs, nchunk * nblk, eg, HEAD_DIM), f32),
        compiler_params=_params(("arbitrary", "arbitrary"), 40),
        name="compress_sample",
    )(page_table, cache.reshape(n_pool, PAGE_SIZE, eg, HEAD_DIM), pe8, w1cat, w2cat)


def _cmp_attn_kernel(pos_base, pos_step, n_blk, as_bias, per_seq, q_ref, ckv_ref, o_ref, sel_ref, *rest):
    if as_bias:
        ck_s, sem = rest
    else:
        ids_ref, cnt_ref, ck_s, sem = rest
    i = pl.program_id(0)
    tq = q_ref.shape[0]
    nb = ckv_ref.shape[1]
    nc = 2 * nb

    def copies(seq, sl):
        return [pltpu.make_async_copy(ckv_ref.at[seq, :, par, eg, :], ck_s.at[sl, eg, pl.ds(par * nb, nb), :],
                                      sem.at[sl])
                for eg in range(2 * N_KV) for par in range(2)]

    if per_seq:
        slot = i % 2

        @pl.when(i == 0)
        def _():
            for cp in copies(i, slot):
                cp.start()

        @pl.when(i + 1 < pl.num_programs(0))
        def _():
            for cp in copies(i + 1, 1 - slot):
                cp.start()

        for cp in copies(i, slot):
            cp.wait()
    else:
        slot = 0

        @pl.when(i == 0)
        def _():
            for cp in copies(0, 0):
                cp.start()
            for cp in copies(0, 0):
                cp.wait()

    width = sel_ref.shape[-1] if as_bias else pl.cdiv(n_blk, HEAD_DIM) * HEAD_DIM
    m_rows = GRP * tq
    pos0 = pos_base + i * pos_step
    pos_col = pos0 + lax.broadcasted_iota(jnp.int32, (tq, 1), 0)
    pos_rows = jnp.concatenate([pos_col] * GRP, axis=0)
    col = lax.broadcasted_iota(jnp.int32, (m_rows, nc), 1)
    c_idx = 2 * (col % nb) + col // nb
    readable = ((c_idx + 1) * L_CMP - 1) <= pos_rows

    lane = lax.broadcasted_iota(jnp.int32, (tq, width), 1).astype(f32)
    jt = (pos_col // L_SLC).astype(f32)
    forced = (lane == 0.0) | ((lane >= jt - (N_LOCAL - 1)) & (lane <= jt))
    future = lane > jt
    exists = lane < float(n_blk)

    scores = []
    for g in range(N_KV):
        ck = ck_s[slot, g].astype(bf16)
        cv = ck_s[slot, N_KV + g].astype(bf16)
        qg = jnp.concatenate([q_ref[:, (g * GRP + r) * HEAD_DIM:(g * GRP + r + 1) * HEAD_DIM]
                              for r in range(GRP)], axis=0).astype(bf16)
        s = lax.dot_general(qg, ck, (((1,), (1,)), ((), ())), preferred_element_type=f32) * SCALE
        s = jnp.where(readable, s, -jnp.inf)
        mx = jnp.max(s, axis=-1, keepdims=True)
        mx = jnp.where(mx > -jnp.inf, mx, 0.0)
        ex = jnp.where(readable, jnp.exp(s - mx), 0.0)
        p = ex / jnp.maximum(jnp.sum(ex, axis=-1, keepdims=True), 1e-30)
        o = jnp.dot(p.astype(bf16), cv, preferred_element_type=f32)
        for r in range(GRP):
            o_ref[:, (g * GRP + r) * HEAD_DIM:(g * GRP + r + 1) * HEAD_DIM] = o[r * tq:(r + 1) * tq]

        pr = p[0:tq]
        for r in range(1, GRP):
            pr = pr + p[r * tq:(r + 1) * tq]
        ps = pr[:, :nb] + pr[:, nb:]
        if width > nb:
            ps = jnp.concatenate([ps, jnp.zeros((tq, width - nb), f32)], axis=1)
        scores.append(jnp.where(future, -jnp.inf, jnp.where(forced, FORCE, ps)))

    score = jnp.concatenate(scores, axis=0)
    lane = jnp.concatenate([lane] * N_KV, axis=0)
    exists = jnp.concatenate([exists] * N_KV, axis=0)
    removed = jnp.logical_not(exists)
    for k in range(N_SEL):
        se = jnp.where(removed, -jnp.inf, score)
        top = jnp.max(se, axis=-1, keepdims=True)
        cand = jnp.where((se == top) & jnp.logical_not(removed), lane, float(width))
        idx = jnp.min(cand, axis=-1, keepdims=True)
        removed = removed | (lane == idx)
    selected = removed & exists
    if as_bias:
        for g in range(N_KV):
            sel_ref[g] = jnp.where(selected[g * tq:(g + 1) * tq], 0.0, NEG).astype(bf16)
        return

    blk = lax.broadcasted_iota(jnp.int32, (tq, width), 1)
    half_id = (blk // 2).astype(bf16)
    par_id = (blk % 2).astype(bf16)
    before = (lax.broadcasted_iota(jnp.int32, (width, width), 0)
              < lax.broadcasted_iota(jnp.int32, (width, width), 1)).astype(bf16)
    slot_row = lax.broadcasted_iota(jnp.int32, (SLC_SLOTS, width), 0).astype(f32)
    nt = (((1,), (1,)), ((), ()))
    for g in range(N_KV):
        mem = selected[g * tq:(g + 1) * tq].astype(f32)
        uni = jnp.broadcast_to(jnp.max(mem, axis=0, keepdims=True), (tq, width))
        slot_of = jnp.dot(uni.astype(bf16), before, preferred_element_type=f32)
        count = jnp.sum(uni, axis=-1, keepdims=True)
        place = ((slot_row == slot_of[0:1, :]) & (uni[0:1, :] > 0.0)).astype(bf16)
        sel_slot = lax.dot_general(mem.astype(bf16), place, nt, preferred_element_type=f32)
        ids = (2.0 * lax.dot_general(half_id, place, nt, preferred_element_type=f32)
               + lax.dot_general(par_id, place, nt, preferred_element_type=f32))
        cols = slice(g * SLC_SLOTS, (g + 1) * SLC_SLOTS)
        sel_ref[:, cols] = jnp.where(sel_slot > 0.5, 0.0, NEG)
        ids_ref[:, cols] = ids.astype(jnp.int32)
        cnt_ref[:, cols] = jnp.broadcast_to(count, (tq, SLC_SLOTS)).astype(jnp.int32)


def _cmp_attn(z, ckv, tq, pos_base, pos_step, n_blk, as_bias):
    T = z.shape[0]
    n_seq, nc = ckv.shape[0], ckv.shape[1]
    nb = nc // 2
    per_seq = n_seq > 1
    assert n_seq == (T // tq if per_seq else 1)
    if as_bias:
        sel_specs = [pl.BlockSpec((N_KV, tq, n_blk), lambda i: (0, i, 0))]
        sel_shapes = [jax.ShapeDtypeStruct((N_KV, T, n_blk), bf16)]
    else:
        assert tq * N_SEL == SLC_SLOTS
        sel_specs = [pl.BlockSpec((tq, N_KV * SLC_SLOTS), lambda i: (i, 0))] * 3
        sel_shapes = [jax.ShapeDtypeStruct((T, N_KV * SLC_SLOTS), dt) for dt in (f32, jnp.int32, jnp.int32)]
    kern = functools.partial(_cmp_attn_kernel, pos_base, pos_step, n_blk, as_bias, per_seq)
    return pl.pallas_call(
        kern,
        grid=(T // tq,),
        in_specs=[pl.BlockSpec((tq, N_HEADS * HEAD_DIM), lambda i: (i, COL_Q)),
                  pl.BlockSpec(memory_space=pl.ANY)],
        out_specs=[pl.BlockSpec((tq, N_HEADS * HEAD_DIM), lambda i: (i, 0))] + sel_specs,
        out_shape=[jax.ShapeDtypeStruct((T, N_HEADS * HEAD_DIM), f32)] + sel_shapes,
        scratch_shapes=[pltpu.VMEM((2 if per_seq else 1, 2 * N_KV, nc, HEAD_DIM), f32),
                        pltpu.SemaphoreType.DMA((2 if per_seq else 1,))],
        compiler_params=_params(("arbitrary",), 40),
        name="cmp_attn",
    )(z, ckv.reshape(n_seq, nb, 2, 2 * N_KV, HEAD_DIM))


FLASH_T = 256
SLC_TK = 1024
WIN_TK = WINDOW + FLASH_T
CAST_ROWS = 512
M_INIT = -3e38
LOG2E = 1.4426950408889634


def _flash_kernel(windowed, tk, q_ref, k_ref, v_ref, *rest):
    if windowed:
        o_ref, kb_s, vb_s, qa_s = rest
    else:
        bias_ref, oh_ref, o_ref, kb_s, vb_s, qa_s = rest
    tq = q_ref.shape[0]
    T = k_ref.shape[0]
    m_rows = GRP * tq
    qi = pl.program_id(1)

    @pl.when(qi == 0)
    def _():
        def cast(c, _):
            r0 = pl.multiple_of(c * CAST_ROWS, CAST_ROWS)
            kb_s[pl.ds(r0, CAST_ROWS), 0:HEAD_DIM] = k_ref[pl.ds(r0, CAST_ROWS), :].astype(bf16)
            if not windowed:
                kb_s[pl.ds(r0, CAST_ROWS), HEAD_DIM:2 * HEAD_DIM] = oh_ref[pl.ds(r0, CAST_ROWS), :]
            vb_s[pl.ds(r0, CAST_ROWS), :] = v_ref[pl.ds(r0, CAST_ROWS), :].astype(bf16)
            return 0
        lax.fori_loop(0, T // CAST_ROWS, cast, 0)

    for r in range(GRP):
        qa_s[r * tq:(r + 1) * tq, 0:HEAD_DIM] = (
            q_ref[:, r * HEAD_DIM:(r + 1) * HEAD_DIM] * (SCALE * LOG2E)).astype(bf16)
        if not windowed:
            qa_s[r * tq:(r + 1) * tq, HEAD_DIM:2 * HEAD_DIM] = bias_ref[0]
    def tile(k0, carry, masked):
        m, l, acc = carry
        s = lax.dot_general(qa_s[...], kb_s[pl.ds(k0, tk), :], (((1,), (1,)), ((), ())),
                            preferred_element_type=f32)
        if masked:
            t_loc = lax.broadcasted_iota(jnp.int32, (m_rows, 1), 0) % tq
            d = (qi * tq + t_loc - k0) - lax.broadcasted_iota(jnp.int32, (m_rows, tk), 1)
            ok = (d >= 0) & (d <= WINDOW) if windowed else d >= 0
            s = jnp.where(ok, s, NEG)
        m_new = jnp.maximum(m, jnp.max(s, axis=-1, keepdims=True))
        alpha = jnp.exp2(m - m_new)
        p = jnp.exp2(s - m_new)
        l = alpha * l + jnp.sum(p, axis=-1, keepdims=True)
        acc = alpha * acc + jnp.dot(p.astype(bf16), vb_s[pl.ds(k0, tk), :], preferred_element_type=f32)
        return m_new, l, acc

    carry = (jnp.full((m_rows, 1), M_INIT, f32), jnp.zeros((m_rows, 1), f32), jnp.zeros((m_rows, HEAD_DIM), f32))
    if windowed:
        carry = tile(pl.multiple_of(jnp.maximum(qi * tq - WINDOW, 0), tq), carry, True)
    else:
        n_full = (qi * tq) // tk
        carry = lax.fori_loop(0, n_full, lambda kt, c: tile(pl.multiple_of(kt * tk, tk), c, False), carry)
        carry = tile(pl.multiple_of(n_full * tk, tk), carry, True)
    _, l, acc = carry
    o = acc / l
    for r in range(GRP):
        o_ref[:, r * HEAD_DIM:(r + 1) * HEAD_DIM] = o[r * tq:(r + 1) * tq]


def _flash_prompt(z, branch, bias=None):
    T = z.shape[0]
    tq = FLASH_T
    windowed = bias is None
    tk = WIN_TK if windowed else SLC_TK
    m_rows = GRP * tq
    assert WINDOW % tq == 0 and T % SLC_TK == 0 and SLC_TK % tq == 0 and T >= WIN_TK
    kcol = COL_KV // HEAD_DIM + branch * 2 * N_KV
    in_specs = [pl.BlockSpec((tq, GRP * HEAD_DIM), lambda g, i: (i, g)),
                pl.BlockSpec((T, HEAD_DIM), lambda g, i: (0, kcol + g)),
                pl.BlockSpec((T, HEAD_DIM), lambda g, i: (0, kcol + N_KV + g))]
    args = [z, z, z]
    kw = HEAD_DIM
    if not windowed:
        n_blk = bias.shape[-1]
        onehot = (jnp.arange(T)[:, None] // L_SLC == jnp.arange(n_blk)[None, :]).astype(bf16)
        in_specs += [pl.BlockSpec((1, tq, n_blk), lambda g, i: (g, i, 0)), _const_spec((T, n_blk))]
        args += [bias, onehot]
        kw = HEAD_DIM + n_blk
    return pl.pallas_call(
        functools.partial(_flash_kernel, windowed, tk),
        grid=(N_KV, T // tq),
        in_specs=in_specs,
        out_specs=pl.BlockSpec((tq, GRP * HEAD_DIM), lambda g, i: (i, g)),
        out_shape=jax.ShapeDtypeStruct((T, N_HEADS * HEAD_DIM), f32),
        scratch_shapes=[pltpu.VMEM((T, kw), bf16), pltpu.VMEM((T, HEAD_DIM), bf16),
                        pltpu.VMEM((m_rows, kw), bf16)],
        compiler_params=_params(("arbitrary", "arbitrary"), 52),
        name="flash_win" if windowed else "flash_slc",
    )(*args)


SLC_SLOTS = 128
SLC_TILE = 16


def _slc_sample_kernel(nb_past, pos_base, ids_ref, cnt_ref, pt_ref, q_ref, bias_ref, pool_ref, newb_ref, oh_ref,
                       o_ref, kbuf, vbuf, sem):
    step = pl.program_id(0) * N_KV + pl.program_id(1)
    total = pl.num_programs(0) * N_KV
    ts = q_ref.shape[0]
    n_pages = pt_ref.shape[1]
    nbn = newb_ref.shape[1] // L_SLC
    blocks_per_page = PAGE_SIZE // L_SLC
    tk = SLC_TILE * L_SLC
    m_rows = GRP * ts
    sl = step % 2

    def schedule(stp, buf):
        bb = stp // N_KV
        gg = stp % N_KV

        def start_slot(s, _):
            bid = ids_ref[stp * SLC_SLOTS + s]
            rows_dst = pl.ds(pl.multiple_of(s * L_SLC, L_SLC), L_SLC)

            @pl.when(bid < nb_past)
            def _():
                page = pt_ref[bb, jnp.minimum(bid // blocks_per_page, n_pages - 1)]
                rows = pl.ds(pl.multiple_of((bid % blocks_per_page) * L_SLC, L_SLC), L_SLC)
                pltpu.make_async_copy(pool_ref.at[page, rows, gg, :], kbuf.at[buf, rows_dst, :], sem.at[buf]).start()
                pltpu.make_async_copy(pool_ref.at[page, rows, N_KV + gg, :], vbuf.at[buf, rows_dst, :],
                                      sem.at[buf]).start()

            @pl.when(bid >= nb_past)
            def _():
                nid = jnp.clip(bid - nb_past, 0, nbn - 1)
                rows = pl.ds(pl.multiple_of(nid * L_SLC, L_SLC), L_SLC)
                pltpu.make_async_copy(newb_ref.at[bb, rows, gg, :], kbuf.at[buf, rows_dst, :], sem.at[buf]).start()
                pltpu.make_async_copy(newb_ref.at[bb, rows, N_KV + gg, :], vbuf.at[buf, rows_dst, :],
                                      sem.at[buf]).start()
            return 0

        lax.fori_loop(0, cnt_ref[stp], start_slot, 0)

    @pl.when(step == 0)
    def _():
        kbuf[...] = jnp.zeros(kbuf.shape, f32)
        vbuf[...] = jnp.zeros(vbuf.shape, f32)
        schedule(step, sl)

    @pl.when(step + 1 < total)
    def _():
        schedule(step + 1, 1 - sl)

    n_used = cnt_ref[step]

    def drain(c, _):
        pltpu.make_async_copy(newb_ref.at[0, pl.ds(0, L_SLC), 0, :], kbuf.at[sl, pl.ds(0, L_SLC), :], sem.at[sl]).wait()
        pltpu.make_async_copy(newb_ref.at[0, pl.ds(0, L_SLC), 0, :], vbuf.at[sl, pl.ds(0, L_SLC), :], sem.at[sl]).wait()
        return 0

    lax.fori_loop(0, n_used, drain, 0)

    bias = bias_ref[...].astype(bf16)
    qa = jnp.concatenate(
        [jnp.concatenate([(q_ref[:, r * HEAD_DIM:(r + 1) * HEAD_DIM] * (SCALE * LOG2E)).astype(bf16), bias], axis=1)
         for r in range(GRP)], axis=0)

    key_lane = lax.broadcasted_iota(jnp.int32, (1, tk), 1)
    l_minus_t = (lax.broadcasted_iota(jnp.int32, (m_rows, tk), 1) % L_SLC
                 - lax.broadcasted_iota(jnp.int32, (m_rows, tk), 0) % ts)

    def tile(kt, carry):
        m, l, acc = carry
        k0 = pl.multiple_of(kt * tk, tk)
        lim = jnp.zeros((1, tk), jnp.int32)
        for j in range(SLC_TILE):
            bid = ids_ref[step * SLC_SLOTS + kt * SLC_TILE + j]
            lim = jnp.where(key_lane // L_SLC == j, pos_base - bid * L_SLC, lim)
        kb = jnp.concatenate([kbuf[sl, pl.ds(k0, tk), :].astype(bf16), oh_ref[pl.ds(k0, tk), :]], axis=1)
        s = lax.dot_general(qa, kb, (((1,), (1,)), ((), ())), preferred_element_type=f32)
        s = jnp.where(l_minus_t <= lim, s, NEG)
        m_new = jnp.maximum(m, jnp.max(s, axis=-1, keepdims=True))
        alpha = jnp.exp2(m - m_new)
        p = jnp.exp2(s - m_new)
        l = alpha * l + jnp.sum(p, axis=-1, keepdims=True)
        acc = alpha * acc + jnp.dot(p.astype(bf16), vbuf[sl, pl.ds(k0, tk), :].astype(bf16),
                                    preferred_element_type=f32)
        return m_new, l, acc

    carry = (jnp.full((m_rows, 1), M_INIT, f32), jnp.zeros((m_rows, 1), f32), jnp.zeros((m_rows, HEAD_DIM), f32))
    _, l, acc = lax.fori_loop(0, (n_used + SLC_TILE - 1) // SLC_TILE, tile, carry)
    o = acc / l
    for r in range(GRP):
        o_ref[:, r * HEAD_DIM:(r + 1) * HEAD_DIM] = o[r * ts:(r + 1) * ts]


def _slc_sample(zs, bias, ids, cnt, page_table, pool, newb, ts, pos_base):
    Bs = page_table.shape[0]
    n_pool = pool.shape[0]
    nb_past = page_table.shape[1] * (PAGE_SIZE // L_SLC)
    assert ts * N_SEL == SLC_SLOTS and SLC_SLOTS % SLC_TILE == 0
    rows = SLC_SLOTS * L_SLC
    onehot = (jnp.arange(rows)[:, None] // L_SLC == jnp.arange(SLC_SLOTS)[None, :]).astype(bf16)
    ids_flat = ids.reshape(Bs, ts, N_KV * SLC_SLOTS)[:, 0].reshape(-1)
    cnt_flat = cnt.reshape(Bs, ts, N_KV, SLC_SLOTS)[:, 0, :, 0].reshape(-1)
    grid_spec = pltpu.PrefetchScalarGridSpec(
        num_scalar_prefetch=3,
        grid=(Bs, N_KV),
        in_specs=[pl.BlockSpec((ts, GRP * HEAD_DIM), lambda b, g, *_: (b, g)),
                  pl.BlockSpec((ts, SLC_SLOTS), lambda b, g, *_: (b, g)),
                  pl.BlockSpec(memory_space=pl.ANY), pl.BlockSpec(memory_space=pl.ANY),
                  pl.BlockSpec((rows, SLC_SLOTS), lambda b, g, *_: (0, 0))],
        out_specs=pl.BlockSpec((ts, GRP * HEAD_DIM), lambda b, g, *_: (b, g)),
        scratch_shapes=[pltpu.VMEM((2, rows, HEAD_DIM), f32), pltpu.VMEM((2, rows, HEAD_DIM), f32),
                        pltpu.SemaphoreType.DMA((2,))],
    )
    return pl.pallas_call(
        functools.partial(_slc_sample_kernel, nb_past, pos_base),
        grid_spec=grid_spec,
        out_shape=jax.ShapeDtypeStruct((Bs * ts, N_HEADS * HEAD_DIM), f32),
        compiler_params=_params(("arbitrary", "arbitrary"), 40),
        name="slc_sample",
    )(ids_flat, cnt_flat, page_table, zs, bias, pool.reshape(n_pool, PAGE_SIZE, 2 * N_KV, HEAD_DIM), newb, onehot)


def _win_sample_kernel(q_ref, new_ref, win_ref, o_ref, kall, sem):
    b = pl.program_id(0)
    ts = q_ref.shape[0]
    wb = win_ref.shape[1]
    rows = kall.shape[1]

    def copies():
        return [pltpu.make_async_copy(win_ref.at[b, :, eg, :], kall.at[eg, pl.ds(0, wb), :], sem.at[0])
                for eg in range(2 * N_KV)]

    for cp in copies():
        cp.start()
    for eg in range(2 * N_KV):
        kall[eg, wb:wb + ts, :] = new_ref[:, eg * HEAD_DIM:(eg + 1) * HEAD_DIM]
        kall[eg, wb + ts:rows, :] = jnp.zeros((rows - wb - ts, HEAD_DIM), f32)
    for cp in copies():
        cp.wait()

    m_rows = GRP * ts
    tq = lax.broadcasted_iota(jnp.int32, (m_rows, rows), 0) % ts
    ki = lax.broadcasted_iota(jnp.int32, (m_rows, rows), 1)
    d = tq + wb - ki
    ok = (d >= 0) & (d <= WINDOW) & (ki < wb + ts)
    for g in range(N_KV):
        qg = jnp.concatenate([q_ref[:, (g * GRP + r) * HEAD_DIM:(g * GRP + r + 1) * HEAD_DIM]
                              for r in range(GRP)], axis=0).astype(bf16)
        s = lax.dot_general(qg, kall[g].astype(bf16), (((1,), (1,)), ((), ())),
                            preferred_element_type=f32) * SCALE
        s = jnp.where(ok, s, -jnp.inf)
        mx = jnp.max(s, axis=-1, keepdims=True)
        mx = jnp.where(mx > -jnp.inf, mx, 0.0)
        ex = jnp.where(ok, jnp.exp(s - mx), 0.0)
        p = ex / jnp.maximum(jnp.sum(ex, axis=-1, keepdims=True), 1e-30)
        o = jnp.dot(p.astype(bf16), kall[N_KV + g].astype(bf16), preferred_element_type=f32)
        for r in range(GRP):
            o_ref[:, (g * GRP + r) * HEAD_DIM:(g * GRP + r + 1) * HEAD_DIM] = o[r * ts:(r + 1) * ts]


def _win_sample(zs, win_buf, ts):
    Bs, wb = win_buf.shape[0], win_buf.shape[1]
    rows = wb + HEAD_DIM
    ncol = 2 * KV_W
    return pl.pallas_call(
        _win_sample_kernel,
        grid=(Bs,),
        in_specs=[pl.BlockSpec((ts, N_HEADS * HEAD_DIM), lambda b: (b, COL_Q)),
                  pl.BlockSpec((ts, ncol), lambda b: (b, (COL_KV + 2 * ncol) // ncol)),
                  pl.BlockSpec(memory_space=pl.ANY)],
        out_specs=pl.BlockSpec((ts, N_HEADS * HEAD_DIM), lambda b: (b, 0)),
        out_shape=jax.ShapeDtypeStruct((Bs * ts, N_HEADS * HEAD_DIM), f32),
        scratch_shapes=[pltpu.VMEM((2 * N_KV, rows, HEAD_DIM), f32), pltpu.SemaphoreType.DMA((1,))],
        compiler_params=_params(("arbitrary",), 32),
        name="win_sample",
    )(zs, zs, win_buf.reshape(Bs, wb, 2 * N_KV, HEAD_DIM))


def _nsa_out_kernel(oc_ref, os_ref, ow_ref, gn_ref, bg_ref, w_ref, y_ref, u_s):
    bgs = jax.nn.sigmoid(bg_ref[...])
    gn = gn_ref[...]
    for h in range(N_HEADS):
        cols = slice(h * HEAD_DIM, (h + 1) * HEAD_DIM)
        o = (bgs[:, h:h + 1] * oc_ref[:, cols] + bgs[:, N_HEADS + h:N_HEADS + h + 1] * os_ref[:, cols]
             + bgs[:, 2 * N_HEADS + h:2 * N_HEADS + h + 1] * ow_ref[:, cols])
        gh = gn[:, cols]
        u_s[:, cols] = (o * (gh * jax.nn.sigmoid(gh))).astype(bf16)
    y_ref[...] = jnp.dot(u_s[...], w_ref[...], preferred_element_type=f32)


def _nsa_out(o_cmp, o_slc, o_win, z, w):
    T = z.shape[0]
    tm = min(T, 256)
    row = lambda i: (i, 0)
    return pl.pallas_call(
        _nsa_out_kernel,
        grid=(T // tm,),
        in_specs=[pl.BlockSpec((tm, D_MODEL), row)] * 3 + [
            pl.BlockSpec((tm, D_MODEL), lambda i: (i, COL_GN // D_MODEL)),
            pl.BlockSpec((tm, TN), lambda i: (i, COL_BG // TN)),
            _const_spec((D_MODEL, D_MODEL))],
        out_specs=pl.BlockSpec((tm, D_MODEL), row),
        out_shape=jax.ShapeDtypeStruct((T, D_MODEL), f32),
        scratch_shapes=[pltpu.VMEM((tm, D_MODEL), bf16)],
        compiler_params=_params(("arbitrary",), 48),
        name="nsa_out",
    )(o_cmp, o_slc, o_win, z, z, w)


def _merge_kernel(yr_ref, yn_ref, m0_ref, m1_ref, x_ref, w_ref, o_ref):
    u = jax.nn.sigmoid(m0_ref[...]) * yr_ref[...] + jax.nn.sigmoid(m1_ref[...]) * yn_ref[...]
    o_ref[...] = x_ref[...] + jnp.dot(u.astype(bf16), w_ref[...], preferred_element_type=f32)


def _merge(y_rnn, y_nsa, z, x, w):
    T = z.shape[0]
    tm = min(T, 256)
    row = lambda i: (i, 0)
    return pl.pallas_call(
        _merge_kernel,
        grid=(T // tm,),
        in_specs=[pl.BlockSpec((tm, D_MODEL), row)] * 2 + [
            pl.BlockSpec((tm, D_MODEL), lambda i: (i, COL_MG // D_MODEL)),
            pl.BlockSpec((tm, D_MODEL), lambda i: (i, COL_MG // D_MODEL + 1)),
            pl.BlockSpec((tm, D_MODEL), row), _const_spec((D_MODEL, D_MODEL))],
        out_specs=pl.BlockSpec((tm, D_MODEL), row),
        out_shape=jax.ShapeDtypeStruct((T, D_MODEL), f32),
        compiler_params=_params(("arbitrary",), 48),
        name="merge_out",
    )(y_rnn, y_nsa, z, z, x, w)


def _ple_kernel(x_ref, p_ref, wp_ref, wg_ref, gf_ref, o_ref):
    x = x_ref[...]
    emb = jnp.dot(p_ref[...].astype(bf16), wp_ref[...], preferred_element_type=f32)
    gate = jax.nn.sigmoid(jnp.dot(x.astype(bf16), wg_ref[...], preferred_element_type=f32))
    x = x + emb * gate
    ms = jnp.mean(x * x, axis=-1, keepdims=True)
    o_ref[...] = x * lax.rsqrt(ms + EPS) * gf_ref[...]


def _ple_norm(x1, p, w_ple, w_gate, g_final):
    T = x1.shape[0]
    tm = min(T, 256)
    row = lambda i: (i, 0)
    return pl.pallas_call(
        _ple_kernel,
        grid=(T // tm,),
        in_specs=[pl.BlockSpec((tm, D_MODEL), row), pl.BlockSpec((tm, PLE_DIM), row),
                  _const_spec((PLE_DIM, D_MODEL)), _const_spec((D_MODEL, D_MODEL)), _const_spec((1, D_MODEL))],
        out_specs=pl.BlockSpec((tm, D_MODEL), row),
        out_shape=jax.ShapeDtypeStruct((T, D_MODEL), f32),
        compiler_params=_params(("arbitrary",), 40),
        name="ple_norm",
    )(x1, p, w_ple, w_gate, g_final)


def kernel(x_prompt, x_sample, cache_cmp_kv, cache_slc_kv, state_win_kv, state_rnn_h, state_rnn_conv, page_table,
           p_prompt, p_sample, g_norm, w_in, w_conv, b_conv, w_rg, b_rg, lam, w_cmp1, w_cmp2, pe_cmp,
           w_rnn_proj, w_nsa_proj, w_out, w_ple, w_ple_gate, g_final):
    B, T, _ = x_prompt.shape
    Bs, Ts, _ = x_sample.shape
    past = page_table.shape[1] * PAGE_SIZE
    assert B == 1 and w_in.shape[0] == 1, "single prompt sequence, single layer"
    assert Ts == 8 and T % 1024 == 0 and T >= CONV_W - 1 and Ts >= CONV_W - 1 and Ts < L_CMP
    w_tail = _w_in_tail(w_in[0])
    rg_w = _rglru_weights(w_conv[0], b_conv[0], w_rg[0], b_rg[0], lam[0], w_rnn_proj[0])

    cos_p, sin_p = _rope_tables(jnp.arange(T))
    zp = _in_proj(x_prompt.reshape(T, D_MODEL), g_norm, w_in[0], w_tail, cos_p, sin_p)
    cos_s, sin_s = _rope_tables(jnp.tile(past + jnp.arange(Ts), Bs))
    zs = _in_proj(x_sample.reshape(Bs * Ts, D_MODEL), g_norm, w_in[0], w_tail, cos_s, sin_s)

    y_rnn_p, h_p = _rglru_prompt(zp, rg_w)
    stpad = jnp.pad(state_rnn_conv[0], ((0, 0), (8 - (CONV_W - 1), 0), (0, 0))).reshape(Bs * Ts, D_RNN)
    h0x = jnp.repeat(state_rnn_h[0], Ts, axis=0)
    y_rnn_s, hr_s = _rglru_sample(zs, stpad, h0x, rg_w)

    cw = _cmp_weights(w_cmp1[0], w_cmp2[0], pe_cmp[0])
    ckv_p = jnp.transpose(_compress_prompt(zp, cw), (3, 2, 0, 1, 4)).reshape(1, T // L_CMP, 2 * N_KV, HEAD_DIM)
    ckv_s = _compress_sample(page_table, cache_cmp_kv[0], cw)
    n_blk_p = -(-T // L_SLC)
    n_blk_s = -(-(past + Ts) // L_SLC)
    o_cmp_p, bias_p = _cmp_attn(zp, ckv_p, FLASH_T, 0, FLASH_T, n_blk_p, True)
    o_cmp_s, bias_s, ids_s, cnt_s = _cmp_attn(zs, ckv_s, Ts, past, 0, n_blk_s, False)
    o_slc_p = _flash_prompt(zp, 1, bias_p)
    o_win_p = _flash_prompt(zp, 2)

    ncol = 2 * KV_W
    nb_past = past // L_SLC
    new_rows = (n_blk_s - nb_past) * L_SLC
    newb = jnp.pad(zs[:, COL_KV + ncol:COL_KV + 2 * ncol].reshape(Bs, Ts, 2 * N_KV, HEAD_DIM),
                   ((0, 0), (0, new_rows - Ts), (0, 0), (0, 0)))
    o_slc_s = _slc_sample(zs, bias_s, ids_s, cnt_s, page_table, cache_slc_kv[0], newb, Ts, past)
    o_win_s = _win_sample(zs, state_win_kv[0], Ts)

    w_nsa_b, w_out_b = w_nsa_proj[0].astype(bf16), w_out[0].astype(bf16)
    w_ple_b, w_gate_b = w_ple[0].astype(bf16), w_ple_gate[0].astype(bf16)
    gf = g_final.reshape(1, D_MODEL)

    def tail(z, x, p, y_rnn, o_cmp, o_slc, o_win):
        y_nsa = _nsa_out(o_cmp, o_slc, o_win, z, w_nsa_b)
        x1 = _merge(y_rnn, y_nsa, z, x, w_out_b)
        return _ple_norm(x1, p, w_ple_b, w_gate_b, gf)

    y_p = tail(zp, x_prompt.reshape(T, D_MODEL), p_prompt.reshape(T, PLE_DIM), y_rnn_p, o_cmp_p, o_slc_p, o_win_p)
    y_s = tail(zs, x_sample.reshape(Bs * Ts, D_MODEL), p_sample.reshape(Bs * Ts, PLE_DIM), y_rnn_s, o_cmp_s, o_slc_s,
               o_win_s)

    kv_shape = (2, N_KV, HEAD_DIM)
    zs3 = zs.reshape(Bs, Ts, D_Z)
    keep_p = min(WINDOW, T)
    win_s = jnp.concatenate([state_win_kv[0], zs3[:, :, COL_KV + 2 * ncol:COL_KV + 3 * ncol].reshape(Bs, Ts, *kv_shape)],
                            axis=1)
    keep_s = min(WINDOW, past + Ts)
    hist = CONV_W - 1
    return (y_p.reshape(1, T, D_MODEL),
            y_s.reshape(Bs, Ts, D_MODEL),
            zp[:, COL_KV:COL_KV + ncol].reshape(1, 1, T, *kv_shape),
            zs3[:, :, COL_KV:COL_KV + ncol].reshape(1, Bs, Ts, *kv_shape),
            zp[:, COL_KV + ncol:COL_KV + 2 * ncol].reshape(1, 1, T, *kv_shape),
            zs3[:, :, COL_KV + ncol:COL_KV + 2 * ncol].reshape(1, Bs, Ts, *kv_shape),
            zp[T - keep_p:, COL_KV + 2 * ncol:COL_KV + 3 * ncol].reshape(1, 1, keep_p, *kv_shape),
            win_s[None, :, win_s.shape[1] - keep_s:],
            h_p[7:8].reshape(1, 1, D_RNN),
            hr_s.reshape(Bs, Ts, D_RNN)[None, :, Ts - 1],
            zp[T - hist:, COL_XR:COL_XR + D_RNN].reshape(1, 1, hist, D_RNN),
            zs3[None, :, Ts - hist:, COL_XR:COL_XR + D_RNN])
```

```python
import functools
import math

import jax
import jax.numpy as jnp
from jax import lax
from jax.experimental import pallas as pl
from jax.experimental.pallas import tpu as pltpu

f32 = jnp.float32
bf16 = jnp.bfloat16

D_MODEL = 2048
D_RNN = 2048
RNN_BLOCKS = 8
RNN_BLK = D_RNN // RNN_BLOCKS
CONV_W = 4
C_SCALE = 8.0
N_HEADS = 16
HEAD_DIM = 128
N_KV = 4
GRP = N_HEADS // N_KV
KV_W = N_KV * HEAD_DIM
L_CMP = 32
L_SLC = 64
PAGE_SIZE = 128
N_SEL = 16
N_LOCAL = 2
WINDOW = 512
FORCE = 1e4
SCALE = HEAD_DIM ** -0.5
ROPE_THETA = 10000.0
PLE_DIM = 256
EPS = 1e-6
NEG = -1e30

COL_Q = 0
COL_XR = 2048
COL_GR = 4096
COL_GN = 6144
COL_MG = 8192
COL_KV = 12288
COL_BG = 15360
D_Z = 15872
TN = 512
MIB = 1024 * 1024


def _params(sem, vmem_mib):
    return pltpu.CompilerParams(dimension_semantics=sem, vmem_limit_bytes=vmem_mib * MIB)


J_XR, J_GN, J_MG, J_KV, J_BG = COL_XR // TN, COL_GN // TN, COL_MG // TN, COL_KV // TN, COL_BG // TN
SRC_XR, SRC_Q, SRC_KV, SRC_GN, SRC_TAIL = 0, 4096 // TN, 6144 // TN, 9216 // TN, 11264


def _from_tail(j):
    return ((j >= J_MG) & (j < J_KV)) | (j >= J_BG)


def _w_src_block(j):
    return jnp.where(j < J_XR, j + SRC_Q,
                     jnp.where(j < J_GN, j - J_XR + SRC_XR,
                               jnp.where(j < J_MG, j - J_GN + SRC_GN,
                                         jnp.where(j < J_KV, J_MG - 1 - J_GN + SRC_GN,
                                                   jnp.where(j < J_BG, j - J_KV + SRC_KV, J_BG - 1 - J_KV + SRC_KV)))))


def _w_tail_block(j):
    n_mg = J_KV - J_MG
    return jnp.where(j < J_MG, 0, jnp.where(j < J_KV, j - J_MG, jnp.where(j < J_BG, n_mg - 1, n_mg)))


def _proj_kernel(x_ref, g_ref, w_ref, wt_ref, cos_ref, sin_ref, o_ref, hn_ref):
    j = pl.program_id(1)

    @pl.when(j == 0)
    def _():
        x = x_ref[...]
        ms = jnp.mean(x * x, axis=-1, keepdims=True)
        hn_ref[...] = (x * lax.rsqrt(ms + EPS) * g_ref[...]).astype(bf16)

    tail = _from_tail(j)
    nt = (((1,), (1,)), ((), ()))

    @pl.when(jnp.logical_not(tail))
    def _():
        o_ref[...] = lax.dot_general(hn_ref[...], w_ref[...].astype(bf16), nt, preferred_element_type=f32)

    @pl.when(tail)
    def _():
        o_ref[...] = lax.dot_general(hn_ref[...], wt_ref[...].astype(bf16), nt, preferred_element_type=f32)

    @pl.when((j < J_XR) | ((j >= J_KV) & (j < J_BG) & ((j - J_KV) % 2 == 0)))
    def _():
        c = cos_ref[...]
        s = sin_ref[...]
        for h in range(TN // HEAD_DIM):
            a = o_ref[:, h * HEAD_DIM:(h + 1) * HEAD_DIM]
            o_ref[:, h * HEAD_DIM:(h + 1) * HEAD_DIM] = a * c + pltpu.roll(a, HEAD_DIM // 2, 1) * s


def _in_proj(x, g_norm, w_in, w_tail, cos_t, sin_t):
    T = x.shape[0]
    tm = min(T, 1024)
    return pl.pallas_call(
        _proj_kernel,
        grid=(T // tm, D_Z // TN),
        in_specs=[
            pl.BlockSpec((tm, D_MODEL), lambda i, j: (i, 0)),
            pl.BlockSpec((1, D_MODEL), lambda i, j: (0, 0)),
            pl.BlockSpec((TN, D_MODEL), lambda i, j: (_w_src_block(j), 0)),
            pl.BlockSpec((TN, D_MODEL), lambda i, j: (_w_tail_block(j), 0)),
            pl.BlockSpec((tm, HEAD_DIM), lambda i, j: (i, 0)),
            pl.BlockSpec((tm, HEAD_DIM), lambda i, j: (i, 0)),
        ],
        out_specs=pl.BlockSpec((tm, TN), lambda i, j: (i, j)),
        out_shape=jax.ShapeDtypeStruct((T, D_Z), f32),
        scratch_shapes=[pltpu.VMEM((tm, D_MODEL), bf16)],
        compiler_params=_params(("arbitrary", "arbitrary"), 52),
        name="in_proj",
    )(x, g_norm, w_in, w_tail, cos_t, sin_t)


def _rope_tables(pos):
    half = HEAD_DIM // 2
    inv = ROPE_THETA ** (-jnp.arange(half, dtype=f32) / half)
    ang = pos.astype(f32)[:, None] * inv[None, :]
    cos = jnp.cos(ang)
    sin = jnp.sin(ang)
    return jnp.concatenate([cos, cos], axis=1), jnp.concatenate([-sin, sin], axis=1)


def _w_in_tail(wt):
    n_bg = 3 * N_HEADS
    pad = jnp.zeros((TN - n_bg, D_MODEL), wt.dtype)
    return jnp.concatenate([wt[SRC_TAIL + n_bg:], wt[SRC_TAIL:SRC_TAIL + n_bg], pad], axis=0)


def _expm1(x):
    u = jnp.exp(x)
    near = jnp.where(u == 1.0, x, (u - 1.0) * x / jnp.log(u))
    return jnp.where(x < -0.5, u - 1.0, near)


def _rglru_core(prompt, xr_ref, gr_ref, st_ref, h0_ref, wconv_ref, bconv_ref, wrg_ref, brg_ref, c_ref,
                wproj_ref, y_ref, hout_ref, a_s, b_s, tail_s, hc_s):
    tm = xr_ref.shape[0]
    groups = tm // 8
    if prompt:
        @pl.when(pl.program_id(0) == 0)
        def _():
            tail_s[...] = jnp.zeros_like(tail_s)
            hc_s[...] = jnp.zeros_like(hc_s)

    row = lax.broadcasted_iota(jnp.int32, (tm, RNN_BLK), 0)
    t8 = row % 8
    tseq = row if prompt else t8
    for n in range(RNN_BLOCKS):
        cols = slice(n * RNN_BLK, (n + 1) * RNN_BLK)
        xr = xr_ref[:, cols]
        xc = jnp.broadcast_to(bconv_ref[:, cols], (tm, RNN_BLK))
        for k in range(CONV_W):
            s = CONV_W - 1 - k
            if s == 0:
                term = xr
            else:
                if prompt:
                    hist = jnp.tile(pltpu.roll(tail_s[:, cols], s, 0), (groups, 1))
                else:
                    hist = pltpu.roll(st_ref[:, cols], tm - 8 + s, 0)
                term = jnp.where(tseq >= s, pltpu.roll(xr, s, 0), hist)
            xc = xc + term * wconv_ref[k:k + 1, cols]
        g = jnp.dot(xc.astype(bf16), wrg_ref[n], preferred_element_type=f32)
        r = jax.nn.sigmoid(g[:, :RNN_BLK] + brg_ref[0:1, cols])
        i = jax.nn.sigmoid(g[:, RNN_BLK:] + brg_ref[1:2, cols])
        log_a = c_ref[:, cols] * r
        a = jnp.exp(log_a)
        b = jnp.sqrt(-_expm1(2.0 * log_a)) * (i * xc)
        for s in (1, 2, 4):
            m = t8 >= s
            a_sh = pltpu.roll(a, s, 0)
            b_sh = pltpu.roll(b, s, 0)
            b = jnp.where(m, a * b_sh + b, b)
            a = jnp.where(m, a * a_sh, a)
        a_s[:, cols] = a
        b_s[:, cols] = b

    if prompt:
        tail_s[...] = xr_ref[tm - 8:tm, :]

        def body(c, h):
            r0 = pl.multiple_of(c * 8, 8)
            hg = a_s[pl.ds(r0, 8), :] * h + b_s[pl.ds(r0, 8), :]
            b_s[pl.ds(r0, 8), :] = hg
            return jnp.broadcast_to(hg[7:8, :], (8, D_RNN))

        h = lax.fori_loop(0, groups, body, hc_s[...])
        hc_s[...] = h
        hout_ref[...] = h
    else:
        b_s[...] = a_s[...] * h0_ref[...] + b_s[...]
        hout_ref[...] = b_s[...]

    gr = gr_ref[...]
    u = (b_s[...] * (gr * jax.nn.sigmoid(gr))).astype(bf16)
    y_ref[...] = jnp.dot(u, wproj_ref[...], preferred_element_type=f32)


def _rglru_prompt_kernel(xr_ref, gr_ref, wconv_ref, bconv_ref, wrg_ref, brg_ref, c_ref, wproj_ref,
                         y_ref, hout_ref, a_s, b_s, tail_s, hc_s):
    _rglru_core(True, xr_ref, gr_ref, None, None, wconv_ref, bconv_ref, wrg_ref, brg_ref, c_ref,
                wproj_ref, y_ref, hout_ref, a_s, b_s, tail_s, hc_s)


def _rglru_sample_kernel(xr_ref, gr_ref, st_ref, h0_ref, wconv_ref, bconv_ref, wrg_ref, brg_ref, c_ref,
                         wproj_ref, y_ref, hout_ref, a_s, b_s):
    _rglru_core(False, xr_ref, gr_ref, st_ref, h0_ref, wconv_ref, bconv_ref, wrg_ref, brg_ref, c_ref,
                wproj_ref, y_ref, hout_ref, a_s, b_s, None, None)


def _const_spec(shape):
    nd = len(shape)
    return pl.BlockSpec(shape, lambda *_: (0,) * nd)


def _rglru_weights(w_conv, b_conv, w_rg, b_rg, lam, w_rnn_proj):
    wrg = jnp.concatenate([w_rg[0], w_rg[1]], axis=-1).astype(bf16)
    c = (-C_SCALE * jax.nn.softplus(-lam.astype(f32))).reshape(1, D_RNN)
    return (w_conv, b_conv.reshape(1, D_RNN), wrg, b_rg, c, w_rnn_proj.astype(bf16))


def _rglru_wspecs():
    return [_const_spec((CONV_W, D_RNN)), _const_spec((1, D_RNN)), _const_spec((RNN_BLOCKS, RNN_BLK, 2 * RNN_BLK)),
            _const_spec((2, D_RNN)), _const_spec((1, D_RNN)), _const_spec((D_RNN, D_MODEL))]


def _rglru_prompt(z, weights):
    T = z.shape[0]
    tm = 256
    return pl.pallas_call(
        _rglru_prompt_kernel,
        grid=(T // tm,),
        in_specs=[pl.BlockSpec((tm, D_RNN), lambda i: (i, COL_XR // D_RNN)),
                  pl.BlockSpec((tm, D_RNN), lambda i: (i, COL_GR // D_RNN))] + _rglru_wspecs(),
        out_specs=[pl.BlockSpec((tm, D_MODEL), lambda i: (i, 0)), _const_spec((8, D_RNN))],
        out_shape=[jax.ShapeDtypeStruct((T, D_MODEL), f32), jax.ShapeDtypeStruct((8, D_RNN), f32)],
        scratch_shapes=[pltpu.VMEM((tm, D_RNN), f32), pltpu.VMEM((tm, D_RNN), f32),
                        pltpu.VMEM((8, D_RNN), f32), pltpu.VMEM((8, D_RNN), f32)],
        compiler_params=_params(("arbitrary",), 56),
        name="rglru_prompt",
    )(z, z, *weights)


def _rglru_sample(z, stpad, h0x, weights):
    T = z.shape[0]
    return pl.pallas_call(
        _rglru_sample_kernel,
        grid=(1,),
        in_specs=[pl.BlockSpec((T, D_RNN), lambda i: (0, COL_XR // D_RNN)),
                  pl.BlockSpec((T, D_RNN), lambda i: (0, COL_GR // D_RNN)),
                  _const_spec((T, D_RNN)), _const_spec((T, D_RNN))] + _rglru_wspecs(),
        out_specs=[_const_spec((T, D_MODEL)), _const_spec((T, D_RNN))],
        out_shape=[jax.ShapeDtypeStruct((T, D_MODEL), f32), jax.ShapeDtypeStruct((T, D_RNN), f32)],
        scratch_shapes=[pltpu.VMEM((T, D_RNN), f32), pltpu.VMEM((T, D_RNN), f32)],
        compiler_params=_params(("arbitrary",), 56),
        name="rglru_sample",
    )(z, z, stpad, h0x, *weights)


def _compress_mlp(load_rows, pe_ref, w1_ref, w2_ref, e):
    acc = None
    for lp in range(L_CMP // 2):
        parts = []
        for dl in range(2):
            l = 2 * lp + dl
            parts.append((load_rows(l) + pe_ref[e, l:l + 1, :]).astype(bf16))
        d = jnp.dot(jnp.concatenate(parts, axis=1), w1_ref[e, lp], preferred_element_type=f32)
        acc = d if acc is None else acc + d
    hid = acc * jax.nn.sigmoid(acc)
    return jnp.dot(hid.astype(bf16), w2_ref[e], preferred_element_type=f32)


def _compress_prompt_kernel(x_ref, pe_ref, w1_ref, w2_ref, o_ref):
    e = pl.program_id(0)
    nb = o_ref.shape[3]

    def load_rows(l):
        return jnp.concatenate([x_ref[pl.ds(l, nb, stride=2 * L_CMP), :],
                                x_ref[pl.ds(L_CMP + l, nb, stride=2 * L_CMP), :]], axis=0)

    res = _compress_mlp(load_rows, pe_ref, w1_ref, w2_ref, e)
    o_ref[0, 0, 0] = res[:nb]
    o_ref[0, 0, 1] = res[nb:]


def _cmp_weights(w_cmp1, w_cmp2, pe_cmp):
    w1 = w_cmp1.reshape(2, L_CMP // 2, 2 * HEAD_DIM, HEAD_DIM).astype(bf16)
    return pe_cmp, w1, w_cmp2.astype(bf16)


def _cmp_wspecs():
    return [_const_spec((2, L_CMP, HEAD_DIM)), _const_spec((2, L_CMP // 2, 2 * HEAD_DIM, HEAD_DIM)),
            _const_spec((2, HEAD_DIM, HEAD_DIM))]


def _compress_prompt(z, cw):
    T = z.shape[0]
    nb = T // (2 * L_CMP)
    col0 = COL_KV // HEAD_DIM
    return pl.pallas_call(
        _compress_prompt_kernel,
        grid=(2, N_KV),
        in_specs=[pl.BlockSpec((T, HEAD_DIM), lambda e, g: (0, col0 + e * N_KV + g))] + _cmp_wspecs(),
        out_specs=pl.BlockSpec((1, 1, 2, nb, HEAD_DIM), lambda e, g: (e, g, 0, 0, 0)),
        out_shape=jax.ShapeDtypeStruct((2, N_KV, 2, nb, HEAD_DIM), f32),
        compiler_params=_params(("arbitrary", "arbitrary"), 32),
        name="compress_prompt",
    )(z, *cw)


CMP_PAGES = 16


def _compress_sample_kernel(pt_ref, cache_ref, pe8_ref, w1_ref, w2_ref, o_ref, buf, sem):
    b = pl.program_id(0)
    c = pl.program_id(1)
    nchunk = pl.num_programs(1)
    step = b * nchunk + c
    total = pl.num_programs(0) * nchunk
    slot = step % 2
    nblk = CMP_PAGES * PAGE_SIZE // L_CMP
    rows = nblk * 2 * N_KV

    def copies(bb, cc, sl):
        return [pltpu.make_async_copy(cache_ref.at[pt_ref[bb, cc * CMP_PAGES + p]],
                                      buf.at[sl, pl.ds(p * PAGE_SIZE, PAGE_SIZE)], sem.at[sl])
                for p in range(CMP_PAGES)]

    @pl.when(step == 0)
    def _():
        for cp in copies(b, c, slot):
            cp.start()

    @pl.when(step + 1 < total)
    def _():
        nxt = step + 1
        for cp in copies(nxt // nchunk, nxt % nchunk, 1 - slot):
            cp.start()

    for cp in copies(b, c, slot):
        cp.wait()

    is_k = lax.broadcasted_iota(jnp.int32, (rows, HEAD_DIM), 0) % (2 * N_KV) < N_KV
    acc = None
    for lp in range(L_CMP // 2):
        parts = []
        for dl in range(2):
            l = 2 * lp + dl
            x = buf[slot, pl.ds(l, nblk, stride=L_CMP), :, :] + pe8_ref[l]
            parts.append(x.reshape(rows, HEAD_DIM).astype(bf16))
        d = jnp.dot(jnp.concatenate(parts, axis=1), w1_ref[lp], preferred_element_type=f32)
        acc = d if acc is None else acc + d
    hid = jnp.where(is_k, acc[:, :HEAD_DIM], acc[:, HEAD_DIM:])
    hid = hid * jax.nn.sigmoid(hid)
    res = jnp.dot(hid.astype(bf16), w2_ref[...], preferred_element_type=f32)
    res = jnp.where(is_k, res[:, :HEAD_DIM], res[:, HEAD_DIM:])
    o_ref[0] = res.reshape(nblk, 2 * N_KV, HEAD_DIM)


def _compress_sample(page_table, cache, cw):
    pe_cmp, w1, w2 = cw
    Bs, n_pages = page_table.shape
    n_pool = cache.shape[0]
    nchunk = n_pages // CMP_PAGES
    nblk = CMP_PAGES * PAGE_SIZE // L_CMP
    eg = 2 * N_KV
    pe8 = jnp.repeat(jnp.transpose(pe_cmp, (1, 0, 2)), N_KV, axis=1)
    w1cat = jnp.concatenate([w1[0], w1[1]], axis=-1)
    w2cat = jnp.concatenate([w2[0], w2[1]], axis=-1)
    grid_spec = pltpu.PrefetchScalarGridSpec(
        num_scalar_prefetch=1,
        grid=(Bs, nchunk),
        in_specs=[pl.BlockSpec(memory_space=pl.ANY),
                  pl.BlockSpec(pe8.shape, lambda b, c, pt: (0, 0, 0)),
                  pl.BlockSpec(w1cat.shape, lambda b, c, pt: (0, 0, 0)),
                  pl.BlockSpec(w2cat.shape, lambda b, c, pt: (0, 0))],
        out_specs=pl.BlockSpec((1, nblk, eg, HEAD_DIM), lambda b, c, pt: (b, c, 0, 0)),
        scratch_shapes=[pltpu.VMEM((2, CMP_PAGES * PAGE_SIZE, eg, HEAD_DIM), f32), pltpu.SemaphoreType.DMA((2,))],
    )
    return pl.pallas_call(
        _compress_sample_kernel,
        grid_spec=grid_spec,
        out_shape=jax.ShapeDtypeStruct((Bs, nchunk * nblk, eg, HEAD_DIM), f32),
        compiler_params=_params(("arbitrary", "arbitrary"), 40),
        name="compress_sample",
    )(page_table, cache.reshape(n_pool, PAGE_SIZE, eg, HEAD_DIM), pe8, w1cat, w2cat)


def _cmp_attn_kernel(pos_base, pos_step, n_blk, as_bias, per_seq, q_ref, ckv_ref, o_ref, sel_ref, *rest):
    if as_bias:
        ck_s, sem = rest
    else:
        ids_ref, cnt_ref, new_ref, ck_s, sem = rest
    i = pl.program_id(0)
    tq = q_ref.shape[0]
    nb = ckv_ref.shape[1]
    nc = 2 * nb

    def copies(seq, sl):
        return [pltpu.make_async_copy(ckv_ref.at[seq, :, par, eg, :], ck_s.at[sl, eg, pl.ds(par * nb, nb), :],
                                      sem.at[sl])
                for eg in range(2 * N_KV) for par in range(2)]

    if per_seq:
        slot = i % 2

        @pl.when(i == 0)
        def _():
            for cp in copies(i, slot):
                cp.start()

        @pl.when(i + 1 < pl.num_programs(0))
        def _():
            for cp in copies(i + 1, 1 - slot):
                cp.start()

        for cp in copies(i, slot):
            cp.wait()
    else:
        slot = 0

        @pl.when(i == 0)
        def _():
            for cp in copies(0, 0):
                cp.start()
            for cp in copies(0, 0):
                cp.wait()

    width = sel_ref.shape[-1] if as_bias else pl.cdiv(n_blk, HEAD_DIM) * HEAD_DIM
    m_rows = GRP * tq
    pos0 = pos_base + i * pos_step
    pos_col = pos0 + lax.broadcasted_iota(jnp.int32, (tq, 1), 0)
    pos_rows = jnp.concatenate([pos_col] * GRP, axis=0)
    col = lax.broadcasted_iota(jnp.int32, (m_rows, nc), 1)
    c_idx = 2 * (col % nb) + col // nb
    readable = ((c_idx + 1) * L_CMP - 1) <= pos_rows

    lane = lax.broadcasted_iota(jnp.int32, (tq, width), 1).astype(f32)
    jt = (pos_col // L_SLC).astype(f32)
    forced = (lane == 0.0) | ((lane >= jt - (N_LOCAL - 1)) & (lane <= jt))
    future = lane > jt
    exists = lane < float(n_blk)

    scores = []
    for g in range(N_KV):
        ck = ck_s[slot, g].astype(bf16)
        cv = ck_s[slot, N_KV + g].astype(bf16)
        qg = jnp.concatenate([q_ref[:, (g * GRP + r) * HEAD_DIM:(g * GRP + r + 1) * HEAD_DIM]
                              for r in range(GRP)], axis=0).astype(bf16)
        s = lax.dot_general(qg, ck, (((1,), (1,)), ((), ())), preferred_element_type=f32) * SCALE
        s = jnp.where(readable, s, -jnp.inf)
        mx = jnp.max(s, axis=-1, keepdims=True)
        mx = jnp.where(mx > -jnp.inf, mx, 0.0)
        ex = jnp.where(readable, jnp.exp(s - mx), 0.0)
        p = ex / jnp.maximum(jnp.sum(ex, axis=-1, keepdims=True), 1e-30)
        o = jnp.dot(p.astype(bf16), cv, preferred_element_type=f32)
        for r in range(GRP):
            o_ref[:, (g * GRP + r) * HEAD_DIM:(g * GRP + r + 1) * HEAD_DIM] = o[r * tq:(r + 1) * tq]

        pr = p[0:tq]
        for r in range(1, GRP):
            pr = pr + p[r * tq:(r + 1) * tq]
        ps = pr[:, :nb] + pr[:, nb:]
        if width > nb:
            ps = jnp.concatenate([ps, jnp.zeros((tq, width - nb), f32)], axis=1)
        scores.append(jnp.where(future, -jnp.inf, jnp.where(forced, FORCE, ps)))

    score = jnp.concatenate(scores, axis=0)
    lane = jnp.concatenate([lane] * N_KV, axis=0)
    exists = jnp.concatenate([exists] * N_KV, axis=0)
    removed = jnp.logical_not(exists)
    for k in range(N_SEL):
        se = jnp.where(removed, -jnp.inf, score)
        top = jnp.max(se, axis=-1, keepdims=True)
        cand = jnp.where((se == top) & jnp.logical_not(removed), lane, float(width))
        idx = jnp.min(cand, axis=-1, keepdims=True)
        removed = removed | (lane == idx)
    selected = removed & exists
    if as_bias:
        for g in range(N_KV):
            sel_ref[g] = jnp.where(selected[g * tq:(g + 1) * tq], 0.0, NEG).astype(bf16)
        return

    nb_past = pos_base // L_SLC
    assert nb_past + SLC_SLOTS <= width
    blk = lax.broadcasted_iota(jnp.int32, (tq, width), 1)
    cached = (blk < nb_past).astype(f32)
    half_id = (blk // 2).astype(bf16)
    par_id = (blk % 2).astype(bf16)
    before = (lax.broadcasted_iota(jnp.int32, (width, width), 0)
              < lax.broadcasted_iota(jnp.int32, (width, width), 1)).astype(bf16)
    slot_row = lax.broadcasted_iota(jnp.int32, (SLC_SLOTS, width), 0).astype(f32)
    nt = (((1,), (1,)), ((), ()))
    for g in range(N_KV):
        mem_all = selected[g * tq:(g + 1) * tq]
        new_ref[:, g * SLC_SLOTS:(g + 1) * SLC_SLOTS] = jnp.where(mem_all[:, nb_past:nb_past + SLC_SLOTS], 0.0, NEG)
        mem = mem_all.astype(f32) * cached
        uni = jnp.broadcast_to(jnp.max(mem, axis=0, keepdims=True), (tq, width))
        slot_of = jnp.dot(uni.astype(bf16), before, preferred_element_type=f32)
        count = jnp.sum(uni, axis=-1, keepdims=True)
        place = ((slot_row == slot_of[0:1, :]) & (uni[0:1, :] > 0.0)).astype(bf16)
        sel_slot = lax.dot_general(mem.astype(bf16), place, nt, preferred_element_type=f32)
        ids = (2.0 * lax.dot_general(half_id, place, nt, preferred_element_type=f32)
               + lax.dot_general(par_id, place, nt, preferred_element_type=f32))
        cols = slice(g * SLC_SLOTS, (g + 1) * SLC_SLOTS)
        sel_ref[:, cols] = jnp.where(sel_slot > 0.5, 0.0, NEG)
        ids_ref[:, cols] = ids.astype(jnp.int32)
        cnt_ref[:, cols] = jnp.broadcast_to(count, (tq, SLC_SLOTS)).astype(jnp.int32)


def _cmp_attn(z, ckv, tq, pos_base, pos_step, n_blk, as_bias):
    T = z.shape[0]
    n_seq, nc = ckv.shape[0], ckv.shape[1]
    nb = nc // 2
    per_seq = n_seq > 1
    assert n_seq == (T // tq if per_seq else 1)
    if as_bias:
        sel_specs = [pl.BlockSpec((N_KV, tq, n_blk), lambda i: (0, i, 0))]
        sel_shapes = [jax.ShapeDtypeStruct((N_KV, T, n_blk), bf16)]
    else:
        assert tq * N_SEL == SLC_SLOTS
        sel_specs = [pl.BlockSpec((tq, N_KV * SLC_SLOTS), lambda i: (i, 0))] * 4
        sel_shapes = [jax.ShapeDtypeStruct((T, N_KV * SLC_SLOTS), dt) for dt in (f32, jnp.int32, jnp.int32, f32)]
    kern = functools.partial(_cmp_attn_kernel, pos_base, pos_step, n_blk, as_bias, per_seq)
    return pl.pallas_call(
        kern,
        grid=(T // tq,),
        in_specs=[pl.BlockSpec((tq, N_HEADS * HEAD_DIM), lambda i: (i, COL_Q)),
                  pl.BlockSpec(memory_space=pl.ANY)],
        out_specs=[pl.BlockSpec((tq, N_HEADS * HEAD_DIM), lambda i: (i, 0))] + sel_specs,
        out_shape=[jax.ShapeDtypeStruct((T, N_HEADS * HEAD_DIM), f32)] + sel_shapes,
        scratch_shapes=[pltpu.VMEM((2 if per_seq else 1, 2 * N_KV, nc, HEAD_DIM), f32),
                        pltpu.SemaphoreType.DMA((2 if per_seq else 1,))],
        compiler_params=_params(("arbitrary",), 40),
        name="cmp_attn",
    )(z, ckv.reshape(n_seq, nb, 2, 2 * N_KV, HEAD_DIM))


FLASH_T = 256
SLC_TK = 1024
WIN_TK = WINDOW + FLASH_T
CAST_ROWS = 512
M_INIT = -3e38
LOG2E = 1.4426950408889634


def _flash_kernel(windowed, tk, q_ref, k_ref, v_ref, *rest):
    if windowed:
        o_ref, kb_s, vb_s, qa_s = rest
    else:
        bias_ref, oh_ref, o_ref, kb_s, vb_s, qa_s = rest
    tq = q_ref.shape[0]
    T = k_ref.shape[0]
    m_rows = GRP * tq
    qi = pl.program_id(1)

    @pl.when(qi == 0)
    def _():
        def cast(c, _):
            r0 = pl.multiple_of(c * CAST_ROWS, CAST_ROWS)
            kb_s[pl.ds(r0, CAST_ROWS), 0:HEAD_DIM] = k_ref[pl.ds(r0, CAST_ROWS), :].astype(bf16)
            if not windowed:
                kb_s[pl.ds(r0, CAST_ROWS), HEAD_DIM:2 * HEAD_DIM] = oh_ref[pl.ds(r0, CAST_ROWS), :]
            vb_s[pl.ds(r0, CAST_ROWS), :] = v_ref[pl.ds(r0, CAST_ROWS), :].astype(bf16)
            return 0
        lax.fori_loop(0, T // CAST_ROWS, cast, 0)

    for r in range(GRP):
        qa_s[r * tq:(r + 1) * tq, 0:HEAD_DIM] = (
            q_ref[:, r * HEAD_DIM:(r + 1) * HEAD_DIM] * (SCALE * LOG2E)).astype(bf16)
        if not windowed:
            qa_s[r * tq:(r + 1) * tq, HEAD_DIM:2 * HEAD_DIM] = bias_ref[0]
    def tile(k0, carry, masked, width=tk):
        m, l, acc = carry
        s = lax.dot_general(qa_s[...], kb_s[pl.ds(k0, width), :], (((1,), (1,)), ((), ())),
                            preferred_element_type=f32)
        if masked:
            t_loc = lax.broadcasted_iota(jnp.int32, (m_rows, 1), 0) % tq
            d = (qi * tq + t_loc - k0) - lax.broadcasted_iota(jnp.int32, (m_rows, width), 1)
            ok = (d >= 0) & (d <= WINDOW) if windowed else d >= 0
            s = jnp.where(ok, s, NEG)
        m_new = jnp.maximum(m, jnp.max(s, axis=-1, keepdims=True))
        alpha = jnp.exp2(m - m_new)
        p = jnp.exp2(s - m_new)
        l = alpha * l + jnp.sum(p, axis=-1, keepdims=True)
        acc = alpha * acc + jnp.dot(p.astype(bf16), vb_s[pl.ds(k0, width), :], preferred_element_type=f32)
        return m_new, l, acc

    carry = (jnp.full((m_rows, 1), M_INIT, f32), jnp.zeros((m_rows, 1), f32), jnp.zeros((m_rows, HEAD_DIM), f32))
    if windowed:
        carry = tile(pl.multiple_of(jnp.maximum(qi * tq - WINDOW, 0), tq), carry, True)
    else:
        n_full = (qi * tq) // tk
        carry = lax.fori_loop(0, n_full, lambda kt, c: tile(pl.multiple_of(kt * tk, tk), c, False), carry)
        k0 = pl.multiple_of(n_full * tk, tk)
        carry = lax.switch(qi % (tk // tq),
                           [functools.partial(tile, k0, masked=True, width=(j + 1) * tq) for j in range(tk // tq)],
                           carry)
    _, l, acc = carry
    o = acc / l
    for r in range(GRP):
        o_ref[:, r * HEAD_DIM:(r + 1) * HEAD_DIM] = o[r * tq:(r + 1) * tq]


def _flash_prompt(z, branch, bias=None):
    T = z.shape[0]
    tq = FLASH_T
    windowed = bias is None
    tk = WIN_TK if windowed else SLC_TK
    m_rows = GRP * tq
    assert WINDOW % tq == 0 and T % SLC_TK == 0 and SLC_TK % tq == 0 and T >= WIN_TK
    kcol = COL_KV // HEAD_DIM + branch * 2 * N_KV
    in_specs = [pl.BlockSpec((tq, GRP * HEAD_DIM), lambda g, i: (i, g)),
                pl.BlockSpec((T, HEAD_DIM), lambda g, i: (0, kcol + g)),
                pl.BlockSpec((T, HEAD_DIM), lambda g, i: (0, kcol + N_KV + g))]
    args = [z, z, z]
    kw = HEAD_DIM
    if not windowed:
        n_blk = bias.shape[-1]
        onehot = (jnp.arange(T)[:, None] // L_SLC == jnp.arange(n_blk)[None, :]).astype(bf16)
        in_specs += [pl.BlockSpec((1, tq, n_blk), lambda g, i: (g, i, 0)), _const_spec((T, n_blk))]
        args += [bias, onehot]
        kw = HEAD_DIM + n_blk
    return pl.pallas_call(
        functools.partial(_flash_kernel, windowed, tk),
        grid=(N_KV, T // tq),
        in_specs=in_specs,
        out_specs=pl.BlockSpec((tq, GRP * HEAD_DIM), lambda g, i: (i, g)),
        out_shape=jax.ShapeDtypeStruct((T, N_HEADS * HEAD_DIM), f32),
        scratch_shapes=[pltpu.VMEM((T, kw), bf16), pltpu.VMEM((T, HEAD_DIM), bf16),
                        pltpu.VMEM((m_rows, kw), bf16)],
        compiler_params=_params(("arbitrary", "arbitrary"), 52),
        name="flash_win" if windowed else "flash_slc",
    )(*args)


SLC_SLOTS = 128
SLC_TILE = 16


def _slc_sample_kernel(nb_past, pos_base, ids_ref, cnt_ref, pt_ref, q_ref, bias_ref, bnew_ref, knew_ref, vnew_ref,
                       pool_ref, oh_ref, o_ref, kbuf, vbuf, sem):
    step = pl.program_id(0) * N_KV + pl.program_id(1)
    total = pl.num_programs(0) * N_KV
    ts = q_ref.shape[0]
    n_pages = pt_ref.shape[1]
    rows_new = knew_ref.shape[2]
    blocks_per_page = PAGE_SIZE // L_SLC
    tk = SLC_TILE * L_SLC
    m_rows = GRP * ts
    sl = step % 2

    def schedule(stp, buf):
        bb = stp // N_KV
        gg = stp % N_KV

        def start_slot(s, _):
            bid = ids_ref[stp * SLC_SLOTS + s]
            rows_dst = pl.ds(pl.multiple_of(s * L_SLC, L_SLC), L_SLC)
            page = pt_ref[bb, jnp.minimum(bid // blocks_per_page, n_pages - 1)]
            rows = pl.ds(pl.multiple_of((bid % blocks_per_page) * L_SLC, L_SLC), L_SLC)
            pltpu.make_async_copy(pool_ref.at[page, rows, gg, :], kbuf.at[buf, rows_dst, :], sem.at[buf]).start()
            pltpu.make_async_copy(pool_ref.at[page, rows, N_KV + gg, :], vbuf.at[buf, rows_dst, :], sem.at[buf]).start()
            return 0

        lax.fori_loop(0, cnt_ref[stp], start_slot, 0)

    @pl.when(step == 0)
    def _():
        kbuf[...] = jnp.zeros(kbuf.shape, f32)
        vbuf[...] = jnp.zeros(vbuf.shape, f32)
        schedule(step, sl)

    @pl.when(step + 1 < total)
    def _():
        schedule(step + 1, 1 - sl)

    n_used = cnt_ref[step]

    def drain(c, _):
        pltpu.make_async_copy(pool_ref.at[0, pl.ds(0, L_SLC), 0, :], kbuf.at[sl, pl.ds(0, L_SLC), :], sem.at[sl]).wait()
        pltpu.make_async_copy(pool_ref.at[0, pl.ds(0, L_SLC), 0, :], vbuf.at[sl, pl.ds(0, L_SLC), :], sem.at[sl]).wait()
        return 0

    lax.fori_loop(0, n_used, drain, 0)

    qs = [(q_ref[:, r * HEAD_DIM:(r + 1) * HEAD_DIM] * (SCALE * LOG2E)).astype(bf16) for r in range(GRP)]
    nt = (((1,), (1,)), ((), ()))

    def with_bias(bias):
        return jnp.concatenate([jnp.concatenate([q, bias.astype(bf16)], axis=1) for q in qs], axis=0)

    def update(carry, s, v):
        m, l, acc = carry
        m_new = jnp.maximum(m, jnp.max(s, axis=-1, keepdims=True))
        alpha = jnp.exp2(m - m_new)
        p = jnp.exp2(s - m_new)
        l = alpha * l + jnp.sum(p, axis=-1, keepdims=True)
        acc = alpha * acc + jnp.dot(p.astype(bf16), v.astype(bf16), preferred_element_type=f32)
        return m_new, l, acc

    qa = with_bias(bias_ref[...])

    def tile(kt, carry):
        k0 = pl.multiple_of(kt * tk, tk)
        kb = jnp.concatenate([kbuf[sl, pl.ds(k0, tk), :].astype(bf16), oh_ref[pl.ds(k0, tk), :]], axis=1)
        s = lax.dot_general(qa, kb, nt, preferred_element_type=f32)
        return update(carry, s, vbuf[sl, pl.ds(k0, tk), :])

    carry = (jnp.full((m_rows, 1), M_INIT, f32), jnp.zeros((m_rows, 1), f32), jnp.zeros((m_rows, HEAD_DIM), f32))
    carry = lax.fori_loop(0, (n_used + SLC_TILE - 1) // SLC_TILE, tile, carry)

    kb = jnp.concatenate([knew_ref[0, 0].astype(bf16), oh_ref[pl.ds(0, rows_new), :]], axis=1)
    s = lax.dot_general(with_bias(bnew_ref[...]), kb, nt, preferred_element_type=f32)
    key = lax.broadcasted_iota(jnp.int32, (m_rows, rows_new), 1)
    t_row = lax.broadcasted_iota(jnp.int32, (m_rows, rows_new), 0) % ts
    s = jnp.where(nb_past * L_SLC + key <= pos_base + t_row, s, NEG)
    _, l, acc = update(carry, s, vnew_ref[0, 0])
    o = acc / l
    for r in range(GRP):
        o_ref[:, r * HEAD_DIM:(r + 1) * HEAD_DIM] = o[r * ts:(r + 1) * ts]


def _slc_sample(zs, bias, ids, cnt, bias_new, page_table, pool, newb, ts, pos_base):
    Bs = page_table.shape[0]
    n_pool = pool.shape[0]
    nb_past = page_table.shape[1] * (PAGE_SIZE // L_SLC)
    rows_new = newb.shape[2]
    assert ts * N_SEL == SLC_SLOTS and SLC_SLOTS % SLC_TILE == 0 and rows_new <= SLC_SLOTS * L_SLC
    assert pos_base >= nb_past * L_SLC
    rows = SLC_SLOTS * L_SLC
    onehot = (jnp.arange(rows)[:, None] // L_SLC == jnp.arange(SLC_SLOTS)[None, :]).astype(bf16)
    ids_flat = ids.reshape(Bs, ts, N_KV * SLC_SLOTS)[:, 0].reshape(-1)
    cnt_flat = cnt.reshape(Bs, ts, N_KV, SLC_SLOTS)[:, 0, :, 0].reshape(-1)
    grid_spec = pltpu.PrefetchScalarGridSpec(
        num_scalar_prefetch=3,
        grid=(Bs, N_KV),
        in_specs=[pl.BlockSpec((ts, GRP * HEAD_DIM), lambda b, g, *_: (b, g)),
                  pl.BlockSpec((ts, SLC_SLOTS), lambda b, g, *_: (b, g)),
                  pl.BlockSpec((ts, SLC_SLOTS), lambda b, g, *_: (b, g)),
                  pl.BlockSpec((1, 1, rows_new, HEAD_DIM), lambda b, g, *_: (b, g, 0, 0)),
                  pl.BlockSpec((1, 1, rows_new, HEAD_DIM), lambda b, g, *_: (b, N_KV + g, 0, 0)),
                  pl.BlockSpec(memory_space=pl.ANY),
                  pl.BlockSpec((rows, SLC_SLOTS), lambda b, g, *_: (0, 0))],
        out_specs=pl.BlockSpec((ts, GRP * HEAD_DIM), lambda b, g, *_: (b, g)),
        scratch_shapes=[pltpu.VMEM((2, rows, HEAD_DIM), f32), pltpu.VMEM((2, rows, HEAD_DIM), f32),
                        pltpu.SemaphoreType.DMA((2,))],
    )
    return pl.pallas_call(
        functools.partial(_slc_sample_kernel, nb_past, pos_base),
        grid_spec=grid_spec,
        out_shape=jax.ShapeDtypeStruct((Bs * ts, N_HEADS * HEAD_DIM), f32),
        compiler_params=_params(("arbitrary", "arbitrary"), 40),
        name="slc_sample",
    )(ids_flat, cnt_flat, page_table, zs, bias, bias_new, newb, newb,
      pool.reshape(n_pool, PAGE_SIZE, 2 * N_KV, HEAD_DIM), onehot)


def _win_sample_kernel(q_ref, new_ref, win_ref, o_ref, kall, sem):
    b = pl.program_id(0)
    ts = q_ref.shape[0]
    wb = win_ref.shape[1]
    rows = kall.shape[1]

    def copies():
        return [pltpu.make_async_copy(win_ref.at[b, :, eg, :], kall.at[eg, pl.ds(0, wb), :], sem.at[0])
                for eg in range(2 * N_KV)]

    for cp in copies():
        cp.start()
    for eg in range(2 * N_KV):
        kall[eg, wb:wb + ts, :] = new_ref[:, eg * HEAD_DIM:(eg + 1) * HEAD_DIM]
        kall[eg, wb + ts:rows, :] = jnp.zeros((rows - wb - ts, HEAD_DIM), f32)
    for cp in copies():
        cp.wait()

    m_rows = GRP * ts
    tq = lax.broadcasted_iota(jnp.int32, (m_rows, rows), 0) % ts
    ki = lax.broadcasted_iota(jnp.int32, (m_rows, rows), 1)
    d = tq + wb - ki
    ok = (d >= 0) & (d <= WINDOW) & (ki < wb + ts)
    for g in range(N_KV):
        qg = jnp.concatenate([q_ref[:, (g * GRP + r) * HEAD_DIM:(g * GRP + r + 1) * HEAD_DIM]
                              for r in range(GRP)], axis=0).astype(bf16)
        s = lax.dot_general(qg, kall[g].astype(bf16), (((1,), (1,)), ((), ())),
                            preferred_element_type=f32) * SCALE
        s = jnp.where(ok, s, -jnp.inf)
        mx = jnp.max(s, axis=-1, keepdims=True)
        mx = jnp.where(mx > -jnp.inf, mx, 0.0)
        ex = jnp.where(ok, jnp.exp(s - mx), 0.0)
        p = ex / jnp.maximum(jnp.sum(ex, axis=-1, keepdims=True), 1e-30)
        o = jnp.dot(p.astype(bf16), kall[N_KV + g].astype(bf16), preferred_element_type=f32)
        for r in range(GRP):
            o_ref[:, (g * GRP + r) * HEAD_DIM:(g * GRP + r + 1) * HEAD_DIM] = o[r * ts:(r + 1) * ts]


def _win_sample(zs, win_buf, ts):
    Bs, wb = win_buf.shape[0], win_buf.shape[1]
    rows = wb + HEAD_DIM
    ncol = 2 * KV_W
    return pl.pallas_call(
        _win_sample_kernel,
        grid=(Bs,),
        in_specs=[pl.BlockSpec((ts, N_HEADS * HEAD_DIM), lambda b: (b, COL_Q)),
                  pl.BlockSpec((ts, ncol), lambda b: (b, (COL_KV + 2 * ncol) // ncol)),
                  pl.BlockSpec(memory_space=pl.ANY)],
        out_specs=pl.BlockSpec((ts, N_HEADS * HEAD_DIM), lambda b: (b, 0)),
        out_shape=jax.ShapeDtypeStruct((Bs * ts, N_HEADS * HEAD_DIM), f32),
        scratch_shapes=[pltpu.VMEM((2 * N_KV, rows, HEAD_DIM), f32), pltpu.SemaphoreType.DMA((1,))],
        compiler_params=_params(("arbitrary",), 32),
        name="win_sample",
    )(zs, zs, win_buf.reshape(Bs, wb, 2 * N_KV, HEAD_DIM))


def _nsa_out_kernel(oc_ref, os_ref, ow_ref, gn_ref, bg_ref, w_ref, y_ref, u_s):
    bgs = jax.nn.sigmoid(bg_ref[...])
    gn = gn_ref[...]
    for h in range(N_HEADS):
        cols = slice(h * HEAD_DIM, (h + 1) * HEAD_DIM)
        o = (bgs[:, h:h + 1] * oc_ref[:, cols] + bgs[:, N_HEADS + h:N_HEADS + h + 1] * os_ref[:, cols]
             + bgs[:, 2 * N_HEADS + h:2 * N_HEADS + h + 1] * ow_ref[:, cols])
        gh = gn[:, cols]
        u_s[:, cols] = (o * (gh * jax.nn.sigmoid(gh))).astype(bf16)
    y_ref[...] = jnp.dot(u_s[...], w_ref[...], preferred_element_type=f32)


def _nsa_out(o_cmp, o_slc, o_win, z, w):
    T = z.shape[0]
    tm = min(T, 256)
    row = lambda i: (i, 0)
    return pl.pallas_call(
        _nsa_out_kernel,
        grid=(T // tm,),
        in_specs=[pl.BlockSpec((tm, D_MODEL), row)] * 3 + [
            pl.BlockSpec((tm, D_MODEL), lambda i: (i, COL_GN // D_MODEL)),
            pl.BlockSpec((tm, TN), lambda i: (i, COL_BG // TN)),
            _const_spec((D_MODEL, D_MODEL))],
        out_specs=pl.BlockSpec((tm, D_MODEL), row),
        out_shape=jax.ShapeDtypeStruct((T, D_MODEL), f32),
        scratch_shapes=[pltpu.VMEM((tm, D_MODEL), bf16)],
        compiler_params=_params(("arbitrary",), 48),
        name="nsa_out",
    )(o_cmp, o_slc, o_win, z, z, w)


def _merge_kernel(yr_ref, yn_ref, m0_ref, m1_ref, x_ref, w_ref, o_ref):
    u = jax.nn.sigmoid(m0_ref[...]) * yr_ref[...] + jax.nn.sigmoid(m1_ref[...]) * yn_ref[...]
    o_ref[...] = x_ref[...] + jnp.dot(u.astype(bf16), w_ref[...], preferred_element_type=f32)


def _merge(y_rnn, y_nsa, z, x, w):
    T = z.shape[0]
    tm = min(T, 256)
    row = lambda i: (i, 0)
    return pl.pallas_call(
        _merge_kernel,
        grid=(T // tm,),
        in_specs=[pl.BlockSpec((tm, D_MODEL), row)] * 2 + [
            pl.BlockSpec((tm, D_MODEL), lambda i: (i, COL_MG // D_MODEL)),
            pl.BlockSpec((tm, D_MODEL), lambda i: (i, COL_MG // D_MODEL + 1)),
            pl.BlockSpec((tm, D_MODEL), row), _const_spec((D_MODEL, D_MODEL))],
        out_specs=pl.BlockSpec((tm, D_MODEL), row),
        out_shape=jax.ShapeDtypeStruct((T, D_MODEL), f32),
        compiler_params=_params(("arbitrary",), 48),
        name="merge_out",
    )(y_rnn, y_nsa, z, z, x, w)


def _ple_kernel(x_ref, p_ref, wp_ref, wg_ref, gf_ref, o_ref):
    x = x_ref[...]
    emb = jnp.dot(p_ref[...].astype(bf16), wp_ref[...], preferred_element_type=f32)
    gate = jax.nn.sigmoid(jnp.dot(x.astype(bf16), wg_ref[...], preferred_element_type=f32))
    x = x + emb * gate
    ms = jnp.mean(x * x, axis=-1, keepdims=True)
    o_ref[...] = x * lax.rsqrt(ms + EPS) * gf_ref[...]


def _ple_norm(x1, p, w_ple, w_gate, g_final):
    T = x1.shape[0]
    tm = min(T, 256)
    row = lambda i: (i, 0)
    return pl.pallas_call(
        _ple_kernel,
        grid=(T // tm,),
        in_specs=[pl.BlockSpec((tm, D_MODEL), row), pl.BlockSpec((tm, PLE_DIM), row),
                  _const_spec((PLE_DIM, D_MODEL)), _const_spec((D_MODEL, D_MODEL)), _const_spec((1, D_MODEL))],
        out_specs=pl.BlockSpec((tm, D_MODEL), row),
        out_shape=jax.ShapeDtypeStruct((T, D_MODEL), f32),
        compiler_params=_params(("arbitrary",), 40),
        name="ple_norm",
    )(x1, p, w_ple, w_gate, g_final)


def kernel(x_prompt, x_sample, cache_cmp_kv, cache_slc_kv, state_win_kv, state_rnn_h, state_rnn_conv, page_table,
           p_prompt, p_sample, g_norm, w_in, w_conv, b_conv, w_rg, b_rg, lam, w_cmp1, w_cmp2, pe_cmp,
           w_rnn_proj, w_nsa_proj, w_out, w_ple, w_ple_gate, g_final):
    B, T, _ = x_prompt.shape
    Bs, Ts, _ = x_sample.shape
    past = page_table.shape[1] * PAGE_SIZE
    assert B == 1 and w_in.shape[0] == 1, "single prompt sequence, single layer"
    assert Ts == 8 and T % 1024 == 0 and T >= CONV_W - 1 and Ts >= CONV_W - 1 and Ts < L_CMP
    w_in_t = jnp.transpose(w_in[0])
    w_tail = _w_in_tail(w_in_t)
    rg_w = _rglru_weights(w_conv[0], b_conv[0], w_rg[0], b_rg[0], lam[0], w_rnn_proj[0])

    cos_p, sin_p = _rope_tables(jnp.arange(T))
    zp = _in_proj(x_prompt.reshape(T, D_MODEL), g_norm, w_in_t, w_tail, cos_p, sin_p)
    cos_s, sin_s = _rope_tables(jnp.tile(past + jnp.arange(Ts), Bs))
    zs = _in_proj(x_sample.reshape(Bs * Ts, D_MODEL), g_norm, w_in_t, w_tail, cos_s, sin_s)

    y_rnn_p, h_p = _rglru_prompt(zp, rg_w)
    stpad = jnp.pad(state_rnn_conv[0], ((0, 0), (8 - (CONV_W - 1), 0), (0, 0))).reshape(Bs * Ts, D_RNN)
    h0x = jnp.repeat(state_rnn_h[0], Ts, axis=0)
    y_rnn_s, hr_s = _rglru_sample(zs, stpad, h0x, rg_w)

    cw = _cmp_weights(w_cmp1[0], w_cmp2[0], pe_cmp[0])
    ckv_p = jnp.transpose(_compress_prompt(zp, cw), (3, 2, 0, 1, 4)).reshape(1, T // L_CMP, 2 * N_KV, HEAD_DIM)
    ckv_s = _compress_sample(page_table, cache_cmp_kv[0], cw)
    n_blk_p = -(-T // L_SLC)
    n_blk_s = -(-(past + Ts) // L_SLC)
    o_cmp_p, bias_p = _cmp_attn(zp, ckv_p, FLASH_T, 0, FLASH_T, n_blk_p, True)
    o_cmp_s, bias_s, ids_s, cnt_s, bnew_s = _cmp_attn(zs, ckv_s, Ts, past, 0, n_blk_s, False)
    o_slc_p = _flash_prompt(zp, 1, bias_p)
    o_win_p = _flash_prompt(zp, 2)

    ncol = 2 * KV_W
    nb_past = past // L_SLC
    new_rows = (n_blk_s - nb_past) * L_SLC
    newb = jnp.pad(jnp.transpose(zs[:, COL_KV + ncol:COL_KV + 2 * ncol].reshape(Bs, Ts, 2 * N_KV, HEAD_DIM),
                                 (0, 2, 1, 3)), ((0, 0), (0, 0), (0, new_rows - Ts), (0, 0)))
    o_slc_s = _slc_sample(zs, bias_s, ids_s, cnt_s, bnew_s, page_table, cache_slc_kv[0], newb, Ts, past)
    o_win_s = _win_sample(zs, state_win_kv[0], Ts)

    w_nsa_b, w_out_b = w_nsa_proj[0].astype(bf16), w_out[0].astype(bf16)
    w_ple_b, w_gate_b = w_ple[0].astype(bf16), w_ple_gate[0].astype(bf16)
    gf = g_final.reshape(1, D_MODEL)

    def tail(z, x, p, y_rnn, o_cmp, o_slc, o_win):
        y_nsa = _nsa_out(o_cmp, o_slc, o_win, z, w_nsa_b)
        x1 = _merge(y_rnn, y_nsa, z, x, w_out_b)
        return _ple_norm(x1, p, w_ple_b, w_gate_b, gf)

    y_p = tail(zp, x_prompt.reshape(T, D_MODEL), p_prompt.reshape(T, PLE_DIM), y_rnn_p, o_cmp_p, o_slc_p, o_win_p)
    y_s = tail(zs, x_sample.reshape(Bs * Ts, D_MODEL), p_sample.reshape(Bs * Ts, PLE_DIM), y_rnn_s, o_cmp_s, o_slc_s,
               o_win_s)

    kv_shape = (2, N_KV, HEAD_DIM)
    zs3 = zs.reshape(Bs, Ts, D_Z)
    keep_p = min(WINDOW, T)
    win_s = jnp.concatenate([state_win_kv[0], zs3[:, :, COL_KV + 2 * ncol:COL_KV + 3 * ncol].reshape(Bs, Ts, *kv_shape)],
                            axis=1)
    keep_s = min(WINDOW, past + Ts)
    hist = CONV_W - 1
    return (y_p.reshape(1, T, D_MODEL),
            y_s.reshape(Bs, Ts, D_MODEL),
            zp[:, COL_KV:COL_KV + ncol].reshape(1, 1, T, *kv_shape),
            zs3[:, :, COL_KV:COL_KV + ncol].reshape(1, Bs, Ts, *kv_shape),
            zp[:, COL_KV + ncol:COL_KV + 2 * ncol].reshape(1, 1, T, *kv_shape),
            zs3[:, :, COL_KV + ncol:COL_KV + 2 * ncol].reshape(1, Bs, Ts, *kv_shape),
            zp[T - keep_p:, COL_KV + 2 * ncol:COL_KV + 3 * ncol].reshape(1, 1, keep_p, *kv_shape),
            win_s[None, :, win_s.shape[1] - keep_s:],
            h_p[7:8].reshape(1, 1, D_RNN),
            hr_s.reshape(Bs, Ts, D_RNN)[None, :, Ts - 1],
            zp[T - hist:, COL_XR:COL_XR + D_RNN].reshape(1, 1, hist, D_RNN),
            zs3[None, :, Ts - hist:, COL_XR:COL_XR + D_RNN])
```

```python
import functools
import math

import jax
import jax.numpy as jnp
from jax import lax
from jax.experimental import pallas as pl
from jax.experimental.pallas import tpu as pltpu

f32 = jnp.float32
bf16 = jnp.bfloat16

D_MODEL = 2048
D_RNN = 2048
RNN_BLOCKS = 8
RNN_BLK = D_RNN // RNN_BLOCKS
CONV_W = 4
C_SCALE = 8.0
N_HEADS = 16
HEAD_DIM = 128
N_KV = 4
GRP = N_HEADS // N_KV
KV_W = N_KV * HEAD_DIM
L_CMP = 32
L_SLC = 64
PAGE_SIZE = 128
N_SEL = 16
N_LOCAL = 2
WINDOW = 512
FORCE = 1e4
SCALE = HEAD_DIM ** -0.5
ROPE_THETA = 10000.0
PLE_DIM = 256
EPS = 1e-6
NEG = -1e30

COL_Q = 0
COL_XR = 2048
COL_GR = 4096
COL_GN = 6144
COL_MG = 8192
COL_KV = 12288
COL_BG = 15360
D_Z = 15872
TN = 512
MIB = 1024 * 1024


def _params(sem, vmem_mib):
    return pltpu.CompilerParams(dimension_semantics=sem, vmem_limit_bytes=vmem_mib * MIB)


J_XR, J_GN, J_MG, J_KV, J_BG = COL_XR // TN, COL_GN // TN, COL_MG // TN, COL_KV // TN, COL_BG // TN
SRC_XR, SRC_Q, SRC_KV, SRC_GN, SRC_TAIL = 0, 4096 // TN, 6144 // TN, 9216 // TN, 11264


def _from_tail(j):
    return ((j >= J_MG) & (j < J_KV)) | (j >= J_BG)


def _w_src_block(j):
    return jnp.where(j < J_XR, j + SRC_Q,
                     jnp.where(j < J_GN, j - J_XR + SRC_XR,
                               jnp.where(j < J_MG, j - J_GN + SRC_GN,
                                         jnp.where(j < J_KV, J_MG - 1 - J_GN + SRC_GN,
                                                   jnp.where(j < J_BG, j - J_KV + SRC_KV, J_BG - 1 - J_KV + SRC_KV)))))


def _w_tail_block(j):
    n_mg = J_KV - J_MG
    return jnp.where(j < J_MG, 0, jnp.where(j < J_KV, j - J_MG, jnp.where(j < J_BG, n_mg - 1, n_mg)))


def _proj_kernel(x_ref, g_ref, w_ref, wt_ref, cos_ref, sin_ref, o_ref, hn_ref):
    j = pl.program_id(1)

    @pl.when(j == 0)
    def _():
        x = x_ref[...]
        ms = jnp.mean(x * x, axis=-1, keepdims=True)
        hn_ref[...] = (x * lax.rsqrt(ms + EPS) * g_ref[...]).astype(bf16)

    tail = _from_tail(j)
    nt = (((1,), (1,)), ((), ()))

    @pl.when(jnp.logical_not(tail))
    def _():
        o_ref[...] = lax.dot_general(hn_ref[...], w_ref[...].astype(bf16), nt, preferred_element_type=f32)

    @pl.when(tail)
    def _():
        o_ref[...] = lax.dot_general(hn_ref[...], wt_ref[...].astype(bf16), nt, preferred_element_type=f32)

    @pl.when((j < J_XR) | ((j >= J_KV) & (j < J_BG) & ((j - J_KV) % 2 == 0)))
    def _():
        c = cos_ref[...]
        s = sin_ref[...]
        for h in range(TN // HEAD_DIM):
            a = o_ref[:, h * HEAD_DIM:(h + 1) * HEAD_DIM]
            o_ref[:, h * HEAD_DIM:(h + 1) * HEAD_DIM] = a * c + pltpu.roll(a, HEAD_DIM // 2, 1) * s


def _in_proj(x, g_norm, w_in, w_tail, cos_t, sin_t):
    T = x.shape[0]
    tm = min(T, 1024)
    return pl.pallas_call(
        _proj_kernel,
        grid=(T // tm, D_Z // TN),
        in_specs=[
            pl.BlockSpec((tm, D_MODEL), lambda i, j: (i, 0)),
            pl.BlockSpec((1, D_MODEL), lambda i, j: (0, 0)),
            pl.BlockSpec((TN, D_MODEL), lambda i, j: (_w_src_block(j), 0)),
            pl.BlockSpec((TN, D_MODEL), lambda i, j: (_w_tail_block(j), 0)),
            pl.BlockSpec((tm, HEAD_DIM), lambda i, j: (i, 0)),
            pl.BlockSpec((tm, HEAD_DIM), lambda i, j: (i, 0)),
        ],
        out_specs=pl.BlockSpec((tm, TN), lambda i, j: (i, j)),
        out_shape=jax.ShapeDtypeStruct((T, D_Z), f32),
        scratch_shapes=[pltpu.VMEM((tm, D_MODEL), bf16)],
        compiler_params=_params(("arbitrary", "arbitrary"), 52),
        name="in_proj",
    )(x, g_norm, w_in, w_tail, cos_t, sin_t)


def _rope_tables(pos):
    half = HEAD_DIM // 2
    inv = ROPE_THETA ** (-jnp.arange(half, dtype=f32) / half)
    ang = pos.astype(f32)[:, None] * inv[None, :]
    cos = jnp.cos(ang)
    sin = jnp.sin(ang)
    return jnp.concatenate([cos, cos], axis=1), jnp.concatenate([-sin, sin], axis=1)


def _w_in_tail(wt):
    n_bg = 3 * N_HEADS
    pad = jnp.zeros((TN - n_bg, D_MODEL), wt.dtype)
    return jnp.concatenate([wt[SRC_TAIL + n_bg:], wt[SRC_TAIL:SRC_TAIL + n_bg], pad], axis=0)


def _expm1(x):
    u = jnp.exp(x)
    near = jnp.where(u == 1.0, x, (u - 1.0) * x / jnp.log(u))
    return jnp.where(x < -0.5, u - 1.0, near)


def _rglru_core(prompt, xr_ref, gr_ref, st_ref, h0_ref, wconv_ref, bconv_ref, wrg_ref, brg_ref, c_ref,
                wproj_ref, y_ref, hout_ref, a_s, b_s, tail_s, hc_s):
    tm = xr_ref.shape[0]
    groups = tm // 8
    if prompt:
        @pl.when(pl.program_id(0) == 0)
        def _():
            tail_s[...] = jnp.zeros_like(tail_s)
            hc_s[...] = jnp.zeros_like(hc_s)

    row = lax.broadcasted_iota(jnp.int32, (tm, RNN_BLK), 0)
    t8 = row % 8
    tseq = row if prompt else t8
    for n in range(RNN_BLOCKS):
        cols = slice(n * RNN_BLK, (n + 1) * RNN_BLK)
        xr = xr_ref[:, cols]
        xc = jnp.broadcast_to(bconv_ref[:, cols], (tm, RNN_BLK))
        for k in range(CONV_W):
            s = CONV_W - 1 - k
            if s == 0:
                term = xr
            else:
                if prompt:
                    hist = jnp.tile(pltpu.roll(tail_s[:, cols], s, 0), (groups, 1))
                else:
                    hist = pltpu.roll(st_ref[:, cols], tm - 8 + s, 0)
                term = jnp.where(tseq >= s, pltpu.roll(xr, s, 0), hist)
            xc = xc + term * wconv_ref[k:k + 1, cols]
        g = jnp.dot(xc.astype(bf16), wrg_ref[n], preferred_element_type=f32)
        r = jax.nn.sigmoid(g[:, :RNN_BLK] + brg_ref[0:1, cols])
        i = jax.nn.sigmoid(g[:, RNN_BLK:] + brg_ref[1:2, cols])
        log_a = c_ref[:, cols] * r
        a = jnp.exp(log_a)
        b = jnp.sqrt(-_expm1(2.0 * log_a)) * (i * xc)
        for s in (1, 2, 4):
            m = t8 >= s
            a_sh = pltpu.roll(a, s, 0)
            b_sh = pltpu.roll(b, s, 0)
            b = jnp.where(m, a * b_sh + b, b)
            a = jnp.where(m, a * a_sh, a)
        a_s[:, cols] = a
        b_s[:, cols] = b

    if prompt:
        tail_s[...] = xr_ref[tm - 8:tm, :]

        def body(c, h):
            r0 = pl.multiple_of(c * 8, 8)
            hg = a_s[pl.ds(r0, 8), :] * h + b_s[pl.ds(r0, 8), :]
            b_s[pl.ds(r0, 8), :] = hg
            return jnp.broadcast_to(hg[7:8, :], (8, D_RNN))

        h = lax.fori_loop(0, groups, body, hc_s[...])
        hc_s[...] = h
        hout_ref[...] = h
    else:
        b_s[...] = a_s[...] * h0_ref[...] + b_s[...]
        hout_ref[...] = b_s[...]

    gr = gr_ref[...]
    u = (b_s[...] * (gr * jax.nn.sigmoid(gr))).astype(bf16)
    y_ref[...] = jnp.dot(u, wproj_ref[...], preferred_element_type=f32).astype(y_ref.dtype)


def _rglru_prompt_kernel(xr_ref, gr_ref, wconv_ref, bconv_ref, wrg_ref, brg_ref, c_ref, wproj_ref,
                         y_ref, hout_ref, a_s, b_s, tail_s, hc_s):
    _rglru_core(True, xr_ref, gr_ref, None, None, wconv_ref, bconv_ref, wrg_ref, brg_ref, c_ref,
                wproj_ref, y_ref, hout_ref, a_s, b_s, tail_s, hc_s)


def _rglru_sample_kernel(xr_ref, gr_ref, st_ref, h0_ref, wconv_ref, bconv_ref, wrg_ref, brg_ref, c_ref,
                         wproj_ref, y_ref, hout_ref, a_s, b_s):
    _rglru_core(False, xr_ref, gr_ref, st_ref, h0_ref, wconv_ref, bconv_ref, wrg_ref, brg_ref, c_ref,
                wproj_ref, y_ref, hout_ref, a_s, b_s, None, None)


def _const_spec(shape):
    nd = len(shape)
    return pl.BlockSpec(shape, lambda *_: (0,) * nd)


def _rglru_weights(w_conv, b_conv, w_rg, b_rg, lam, w_rnn_proj):
    wrg = jnp.concatenate([w_rg[0], w_rg[1]], axis=-1).astype(bf16)
    c = (-C_SCALE * jax.nn.softplus(-lam.astype(f32))).reshape(1, D_RNN)
    return (w_conv, b_conv.reshape(1, D_RNN), wrg, b_rg, c, w_rnn_proj.astype(bf16))


def _rglru_wspecs():
    return [_const_spec((CONV_W, D_RNN)), _const_spec((1, D_RNN)), _const_spec((RNN_BLOCKS, RNN_BLK, 2 * RNN_BLK)),
            _const_spec((2, D_RNN)), _const_spec((1, D_RNN)), _const_spec((D_RNN, D_MODEL))]


def _rglru_prompt(z, weights):
    T = z.shape[0]
    tm = 256
    return pl.pallas_call(
        _rglru_prompt_kernel,
        grid=(T // tm,),
        in_specs=[pl.BlockSpec((tm, D_RNN), lambda i: (i, COL_XR // D_RNN)),
                  pl.BlockSpec((tm, D_RNN), lambda i: (i, COL_GR // D_RNN))] + _rglru_wspecs(),
        out_specs=[pl.BlockSpec((tm, D_MODEL), lambda i: (i, 0)), _const_spec((8, D_RNN))],
        out_shape=[jax.ShapeDtypeStruct((T, D_MODEL), bf16), jax.ShapeDtypeStruct((8, D_RNN), f32)],
        scratch_shapes=[pltpu.VMEM((tm, D_RNN), f32), pltpu.VMEM((tm, D_RNN), f32),
                        pltpu.VMEM((8, D_RNN), f32), pltpu.VMEM((8, D_RNN), f32)],
        compiler_params=_params(("arbitrary",), 56),
        name="rglru_prompt",
    )(z, z, *weights)


def _rglru_sample(z, stpad, h0x, weights):
    T = z.shape[0]
    return pl.pallas_call(
        _rglru_sample_kernel,
        grid=(1,),
        in_specs=[pl.BlockSpec((T, D_RNN), lambda i: (0, COL_XR // D_RNN)),
                  pl.BlockSpec((T, D_RNN), lambda i: (0, COL_GR // D_RNN)),
                  _const_spec((T, D_RNN)), _const_spec((T, D_RNN))] + _rglru_wspecs(),
        out_specs=[_const_spec((T, D_MODEL)), _const_spec((T, D_RNN))],
        out_shape=[jax.ShapeDtypeStruct((T, D_MODEL), bf16), jax.ShapeDtypeStruct((T, D_RNN), f32)],
        scratch_shapes=[pltpu.VMEM((T, D_RNN), f32), pltpu.VMEM((T, D_RNN), f32)],
        compiler_params=_params(("arbitrary",), 56),
        name="rglru_sample",
    )(z, z, stpad, h0x, *weights)


def _compress_mlp(load_rows, pe_ref, w1_ref, w2_ref, e):
    acc = None
    for lp in range(L_CMP // 2):
        parts = []
        for dl in range(2):
            l = 2 * lp + dl
            parts.append((load_rows(l) + pe_ref[e, l:l + 1, :]).astype(bf16))
        d = jnp.dot(jnp.concatenate(parts, axis=1), w1_ref[e, lp], preferred_element_type=f32)
        acc = d if acc is None else acc + d
    hid = acc * jax.nn.sigmoid(acc)
    return jnp.dot(hid.astype(bf16), w2_ref[e], preferred_element_type=f32)


def _compress_prompt_kernel(x_ref, pe_ref, w1_ref, w2_ref, o_ref):
    e = pl.program_id(0)
    nb = o_ref.shape[3]

    def load_rows(l):
        return jnp.concatenate([x_ref[pl.ds(l, nb, stride=2 * L_CMP), :],
                                x_ref[pl.ds(L_CMP + l, nb, stride=2 * L_CMP), :]], axis=0)

    res = _compress_mlp(load_rows, pe_ref, w1_ref, w2_ref, e)
    o_ref[0, 0, 0] = res[:nb]
    o_ref[0, 0, 1] = res[nb:]


def _cmp_weights(w_cmp1, w_cmp2, pe_cmp):
    w1 = w_cmp1.reshape(2, L_CMP // 2, 2 * HEAD_DIM, HEAD_DIM).astype(bf16)
    return pe_cmp, w1, w_cmp2.astype(bf16)


def _cmp_wspecs():
    return [_const_spec((2, L_CMP, HEAD_DIM)), _const_spec((2, L_CMP // 2, 2 * HEAD_DIM, HEAD_DIM)),
            _const_spec((2, HEAD_DIM, HEAD_DIM))]


def _compress_prompt(z, cw):
    T = z.shape[0]
    nb = T // (2 * L_CMP)
    col0 = COL_KV // HEAD_DIM
    return pl.pallas_call(
        _compress_prompt_kernel,
        grid=(2, N_KV),
        in_specs=[pl.BlockSpec((T, HEAD_DIM), lambda e, g: (0, col0 + e * N_KV + g))] + _cmp_wspecs(),
        out_specs=pl.BlockSpec((1, 1, 2, nb, HEAD_DIM), lambda e, g: (e, g, 0, 0, 0)),
        out_shape=jax.ShapeDtypeStruct((2, N_KV, 2, nb, HEAD_DIM), f32),
        compiler_params=_params(("arbitrary", "arbitrary"), 32),
        name="compress_prompt",
    )(z, *cw)


CMP_PAGES = 16


def _compress_sample_kernel(pt_ref, cache_ref, pe8_ref, w1_ref, w2_ref, o_ref, buf, sem):
    b = pl.program_id(0)
    c = pl.program_id(1)
    nchunk = pl.num_programs(1)
    step = b * nchunk + c
    total = pl.num_programs(0) * nchunk
    slot = step % 2
    nblk = CMP_PAGES * PAGE_SIZE // L_CMP
    rows = nblk * 2 * N_KV

    def copies(bb, cc, sl):
        return [pltpu.make_async_copy(cache_ref.at[pt_ref[bb, cc * CMP_PAGES + p]],
                                      buf.at[sl, pl.ds(p * PAGE_SIZE, PAGE_SIZE)], sem.at[sl])
                for p in range(CMP_PAGES)]

    @pl.when(step == 0)
    def _():
        for cp in copies(b, c, slot):
            cp.start()

    @pl.when(step + 1 < total)
    def _():
        nxt = step + 1
        for cp in copies(nxt // nchunk, nxt % nchunk, 1 - slot):
            cp.start()

    for cp in copies(b, c, slot):
        cp.wait()

    is_k = lax.broadcasted_iota(jnp.int32, (rows, HEAD_DIM), 0) % (2 * N_KV) < N_KV
    acc = None
    for lp in range(L_CMP // 2):
        parts = []
        for dl in range(2):
            l = 2 * lp + dl
            x = buf[slot, pl.ds(l, nblk, stride=L_CMP), :, :] + pe8_ref[l]
            parts.append(x.reshape(rows, HEAD_DIM).astype(bf16))
        d = jnp.dot(jnp.concatenate(parts, axis=1), w1_ref[lp], preferred_element_type=f32)
        acc = d if acc is None else acc + d
    hid = jnp.where(is_k, acc[:, :HEAD_DIM], acc[:, HEAD_DIM:])
    hid = hid * jax.nn.sigmoid(hid)
    res = jnp.dot(hid.astype(bf16), w2_ref[...], preferred_element_type=f32)
    res = jnp.where(is_k, res[:, :HEAD_DIM], res[:, HEAD_DIM:])
    o_ref[0] = res.reshape(nblk, 2 * N_KV, HEAD_DIM)


def _compress_sample(page_table, cache, cw):
    pe_cmp, w1, w2 = cw
    Bs, n_pages = page_table.shape
    n_pool = cache.shape[0]
    nchunk = n_pages // CMP_PAGES
    nblk = CMP_PAGES * PAGE_SIZE // L_CMP
    eg = 2 * N_KV
    pe8 = jnp.repeat(jnp.transpose(pe_cmp, (1, 0, 2)), N_KV, axis=1)
    w1cat = jnp.concatenate([w1[0], w1[1]], axis=-1)
    w2cat = jnp.concatenate([w2[0], w2[1]], axis=-1)
    grid_spec = pltpu.PrefetchScalarGridSpec(
        num_scalar_prefetch=1,
        grid=(Bs, nchunk),
        in_specs=[pl.BlockSpec(memory_space=pl.ANY),
                  pl.BlockSpec(pe8.shape, lambda b, c, pt: (0, 0, 0)),
                  pl.BlockSpec(w1cat.shape, lambda b, c, pt: (0, 0, 0)),
                  pl.BlockSpec(w2cat.shape, lambda b, c, pt: (0, 0))],
        out_specs=pl.BlockSpec((1, nblk, eg, HEAD_DIM), lambda b, c, pt: (b, c, 0, 0)),
        scratch_shapes=[pltpu.VMEM((2, CMP_PAGES * PAGE_SIZE, eg, HEAD_DIM), f32), pltpu.SemaphoreType.DMA((2,))],
    )
    return pl.pallas_call(
        _compress_sample_kernel,
        grid_spec=grid_spec,
        out_shape=jax.ShapeDtypeStruct((Bs, nchunk * nblk, eg, HEAD_DIM), f32),
        compiler_params=_params(("arbitrary", "arbitrary"), 40),
        name="compress_sample",
    )(page_table, cache.reshape(n_pool, PAGE_SIZE, eg, HEAD_DIM), pe8, w1cat, w2cat)


def _cmp_attn_kernel(pos_base, pos_step, n_blk, as_bias, per_seq, q_ref, ckv_ref, o_ref, sel_ref, *rest):
    if as_bias:
        ck_s, sem = rest
    else:
        ids_ref, cnt_ref, new_ref, ck_s, sem = rest
    i = pl.program_id(0)
    tq = q_ref.shape[0]
    nb = ckv_ref.shape[1]
    nc = 2 * nb

    def copies(seq, sl):
        return [pltpu.make_async_copy(ckv_ref.at[seq, :, par, eg, :], ck_s.at[sl, eg, pl.ds(par * nb, nb), :],
                                      sem.at[sl])
                for eg in range(2 * N_KV) for par in range(2)]

    if per_seq:
        slot = i % 2

        @pl.when(i == 0)
        def _():
            for cp in copies(i, slot):
                cp.start()

        @pl.when(i + 1 < pl.num_programs(0))
        def _():
            for cp in copies(i + 1, 1 - slot):
                cp.start()

        for cp in copies(i, slot):
            cp.wait()
    else:
        slot = 0

        @pl.when(i == 0)
        def _():
            for cp in copies(0, 0):
                cp.start()
            for cp in copies(0, 0):
                cp.wait()

    width = sel_ref.shape[-1] if as_bias else pl.cdiv(n_blk, HEAD_DIM) * HEAD_DIM
    m_rows = GRP * tq
    pos0 = pos_base + i * pos_step
    pos_col = pos0 + lax.broadcasted_iota(jnp.int32, (tq, 1), 0)
    pos_rows = jnp.concatenate([pos_col] * GRP, axis=0)
    col = lax.broadcasted_iota(jnp.int32, (m_rows, nc), 1)
    c_idx = 2 * (col % nb) + col // nb
    readable = ((c_idx + 1) * L_CMP - 1) <= pos_rows

    lane = lax.broadcasted_iota(jnp.int32, (tq, width), 1).astype(f32)
    jt = (pos_col // L_SLC).astype(f32)
    forced = (lane == 0.0) | ((lane >= jt - (N_LOCAL - 1)) & (lane <= jt))
    future = lane > jt
    exists = lane < float(n_blk)

    scores = []
    for g in range(N_KV):
        ck = ck_s[slot, g].astype(bf16)
        cv = ck_s[slot, N_KV + g].astype(bf16)
        qg = jnp.concatenate([q_ref[:, (g * GRP + r) * HEAD_DIM:(g * GRP + r + 1) * HEAD_DIM]
                              for r in range(GRP)], axis=0).astype(bf16)
        s = lax.dot_general(qg, ck, (((1,), (1,)), ((), ())), preferred_element_type=f32) * SCALE
        s = jnp.where(readable, s, -jnp.inf)
        mx = jnp.max(s, axis=-1, keepdims=True)
        mx = jnp.where(mx > -jnp.inf, mx, 0.0)
        ex = jnp.where(readable, jnp.exp(s - mx), 0.0)
        p = ex / jnp.maximum(jnp.sum(ex, axis=-1, keepdims=True), 1e-30)
        o = jnp.dot(p.astype(bf16), cv, preferred_element_type=f32)
        for r in range(GRP):
            o_ref[:, (g * GRP + r) * HEAD_DIM:(g * GRP + r + 1) * HEAD_DIM] = o[r * tq:(r + 1) * tq].astype(o_ref.dtype)

        pr = p[0:tq]
        for r in range(1, GRP):
            pr = pr + p[r * tq:(r + 1) * tq]
        ps = pr[:, :nb] + pr[:, nb:]
        if width > nb:
            ps = jnp.concatenate([ps, jnp.zeros((tq, width - nb), f32)], axis=1)
        scores.append(jnp.where(future, -jnp.inf, jnp.where(forced, FORCE, ps)))

    score = jnp.concatenate(scores, axis=0)
    lane = jnp.concatenate([lane] * N_KV, axis=0)
    exists = jnp.concatenate([exists] * N_KV, axis=0)
    removed = jnp.logical_not(exists)
    for k in range(N_SEL):
        se = jnp.where(removed, -jnp.inf, score)
        top = jnp.max(se, axis=-1, keepdims=True)
        cand = jnp.where((se == top) & jnp.logical_not(removed), lane, float(width))
        idx = jnp.min(cand, axis=-1, keepdims=True)
        removed = removed | (lane == idx)
    selected = removed & exists
    if as_bias:
        for g in range(N_KV):
            sel_ref[g] = jnp.where(selected[g * tq:(g + 1) * tq], 0.0, NEG).astype(bf16)
        return

    nb_past = pos_base // L_SLC
    assert nb_past + SLC_SLOTS <= width
    blk = lax.broadcasted_iota(jnp.int32, (tq, width), 1)
    cached = (blk < nb_past).astype(f32)
    half_id = (blk // 2).astype(bf16)
    par_id = (blk % 2).astype(bf16)
    before = (lax.broadcasted_iota(jnp.int32, (width, width), 0)
              < lax.broadcasted_iota(jnp.int32, (width, width), 1)).astype(bf16)
    slot_row = lax.broadcasted_iota(jnp.int32, (SLC_SLOTS, width), 0).astype(f32)
    nt = (((1,), (1,)), ((), ()))
    for g in range(N_KV):
        mem_all = selected[g * tq:(g + 1) * tq]
        new_ref[:, g * SLC_SLOTS:(g + 1) * SLC_SLOTS] = jnp.where(mem_all[:, nb_past:nb_past + SLC_SLOTS], 0.0, NEG)
        mem = mem_all.astype(f32) * cached
        uni = jnp.broadcast_to(jnp.max(mem, axis=0, keepdims=True), (tq, width))
        slot_of = jnp.dot(uni.astype(bf16), before, preferred_element_type=f32)
        count = jnp.sum(uni, axis=-1, keepdims=True)
        place = ((slot_row == slot_of[0:1, :]) & (uni[0:1, :] > 0.0)).astype(bf16)
        sel_slot = lax.dot_general(mem.astype(bf16), place, nt, preferred_element_type=f32)
        ids = (2.0 * lax.dot_general(half_id, place, nt, preferred_element_type=f32)
               + lax.dot_general(par_id, place, nt, preferred_element_type=f32))
        cols = slice(g * SLC_SLOTS, (g + 1) * SLC_SLOTS)
        sel_ref[:, cols] = jnp.where(sel_slot > 0.5, 0.0, NEG)
        ids_ref[:, cols] = ids.astype(jnp.int32)
        cnt_ref[:, cols] = jnp.broadcast_to(count, (tq, SLC_SLOTS)).astype(jnp.int32)


def _cmp_attn(z, ckv, tq, pos_base, pos_step, n_blk, as_bias):
    T = z.shape[0]
    n_seq, nc = ckv.shape[0], ckv.shape[1]
    nb = nc // 2
    per_seq = n_seq > 1
    assert n_seq == (T // tq if per_seq else 1)
    if as_bias:
        sel_specs = [pl.BlockSpec((N_KV, tq, n_blk), lambda i: (0, i, 0))]
        sel_shapes = [jax.ShapeDtypeStruct((N_KV, T, n_blk), bf16)]
    else:
        assert tq * N_SEL == SLC_SLOTS
        sel_specs = [pl.BlockSpec((tq, N_KV * SLC_SLOTS), lambda i: (i, 0))] * 4
        sel_shapes = [jax.ShapeDtypeStruct((T, N_KV * SLC_SLOTS), dt) for dt in (f32, jnp.int32, jnp.int32, f32)]
    kern = functools.partial(_cmp_attn_kernel, pos_base, pos_step, n_blk, as_bias, per_seq)
    return pl.pallas_call(
        kern,
        grid=(T // tq,),
        in_specs=[pl.BlockSpec((tq, N_HEADS * HEAD_DIM), lambda i: (i, COL_Q)),
                  pl.BlockSpec(memory_space=pl.ANY)],
        out_specs=[pl.BlockSpec((tq, N_HEADS * HEAD_DIM), lambda i: (i, 0))] + sel_specs,
        out_shape=[jax.ShapeDtypeStruct((T, N_HEADS * HEAD_DIM), bf16 if tq % 16 == 0 else f32)] + sel_shapes,
        scratch_shapes=[pltpu.VMEM((2 if per_seq else 1, 2 * N_KV, nc, HEAD_DIM), f32),
                        pltpu.SemaphoreType.DMA((2 if per_seq else 1,))],
        compiler_params=_params(("arbitrary",), 40),
        name="cmp_attn",
    )(z, ckv.reshape(n_seq, nb, 2, 2 * N_KV, HEAD_DIM))


FLASH_T = 256
SLC_TK = 1024
WIN_TK = WINDOW + FLASH_T
CAST_ROWS = 512
M_INIT = -3e38
LOG2E = 1.4426950408889634


def _flash_kernel(windowed, tk, q_ref, k_ref, v_ref, *rest):
    if windowed:
        o_ref, kb_s, vb_s, qa_s = rest
    else:
        bias_ref, oh_ref, o_ref, kb_s, vb_s, qa_s = rest
    tq = q_ref.shape[0]
    T = k_ref.shape[0]
    m_rows = GRP * tq
    qi = pl.program_id(1)

    @pl.when(qi == 0)
    def _():
        def cast(c, _):
            r0 = pl.multiple_of(c * CAST_ROWS, CAST_ROWS)
            kb_s[pl.ds(r0, CAST_ROWS), 0:HEAD_DIM] = k_ref[pl.ds(r0, CAST_ROWS), :].astype(bf16)
            if not windowed:
                kb_s[pl.ds(r0, CAST_ROWS), HEAD_DIM:2 * HEAD_DIM] = oh_ref[pl.ds(r0, CAST_ROWS), :]
            vb_s[pl.ds(r0, CAST_ROWS), :] = v_ref[pl.ds(r0, CAST_ROWS), :].astype(bf16)
            return 0
        lax.fori_loop(0, T // CAST_ROWS, cast, 0)

    for r in range(GRP):
        qa_s[r * tq:(r + 1) * tq, 0:HEAD_DIM] = (
            q_ref[:, r * HEAD_DIM:(r + 1) * HEAD_DIM] * (SCALE * LOG2E)).astype(bf16)
        if not windowed:
            qa_s[r * tq:(r + 1) * tq, HEAD_DIM:2 * HEAD_DIM] = bias_ref[0]
    def tile(k0, carry, masked):
        m, l, acc = carry
        s = lax.dot_general(qa_s[...], kb_s[pl.ds(k0, tk), :], (((1,), (1,)), ((), ())),
                            preferred_element_type=f32)
        if masked:
            t_loc = lax.broadcasted_iota(jnp.int32, (m_rows, 1), 0) % tq
            d = (qi * tq + t_loc - k0) - lax.broadcasted_iota(jnp.int32, (m_rows, tk), 1)
            ok = (d >= 0) & (d <= WINDOW) if windowed else d >= 0
            s = jnp.where(ok, s, NEG)
        m_new = jnp.maximum(m, jnp.max(s, axis=-1, keepdims=True))
        alpha = jnp.exp2(m - m_new)
        p = jnp.exp2(s - m_new)
        l = alpha * l + jnp.sum(p, axis=-1, keepdims=True)
        acc = alpha * acc + jnp.dot(p.astype(bf16), vb_s[pl.ds(k0, tk), :], preferred_element_type=f32)
        return m_new, l, acc

    carry = (jnp.full((m_rows, 1), M_INIT, f32), jnp.zeros((m_rows, 1), f32), jnp.zeros((m_rows, HEAD_DIM), f32))
    if windowed:
        carry = tile(pl.multiple_of(jnp.maximum(qi * tq - WINDOW, 0), tq), carry, True)
    else:
        n_full = (qi * tq) // tk
        carry = lax.fori_loop(0, n_full, lambda kt, c: tile(pl.multiple_of(kt * tk, tk), c, False), carry)
        carry = tile(pl.multiple_of(n_full * tk, tk), carry, True)
    _, l, acc = carry
    o = (acc / l).astype(o_ref.dtype)
    for r in range(GRP):
        o_ref[:, r * HEAD_DIM:(r + 1) * HEAD_DIM] = o[r * tq:(r + 1) * tq]


def _flash_prompt(z, branch, bias=None):
    T = z.shape[0]
    tq = FLASH_T
    windowed = bias is None
    tk = WIN_TK if windowed else SLC_TK
    m_rows = GRP * tq
    assert WINDOW % tq == 0 and T % SLC_TK == 0 and SLC_TK % tq == 0 and T >= WIN_TK
    kcol = COL_KV // HEAD_DIM + branch * 2 * N_KV
    in_specs = [pl.BlockSpec((tq, GRP * HEAD_DIM), lambda g, i: (i, g)),
                pl.BlockSpec((T, HEAD_DIM), lambda g, i: (0, kcol + g)),
                pl.BlockSpec((T, HEAD_DIM), lambda g, i: (0, kcol + N_KV + g))]
    args = [z, z, z]
    kw = HEAD_DIM
    if not windowed:
        n_blk = bias.shape[-1]
        onehot = (jnp.arange(T)[:, None] // L_SLC == jnp.arange(n_blk)[None, :]).astype(bf16)
        in_specs += [pl.BlockSpec((1, tq, n_blk), lambda g, i: (g, i, 0)), _const_spec((T, n_blk))]
        args += [bias, onehot]
        kw = HEAD_DIM + n_blk
    return pl.pallas_call(
        functools.partial(_flash_kernel, windowed, tk),
        grid=(N_KV, T // tq),
        in_specs=in_specs,
        out_specs=pl.BlockSpec((tq, GRP * HEAD_DIM), lambda g, i: (i, g)),
        out_shape=jax.ShapeDtypeStruct((T, N_HEADS * HEAD_DIM), bf16),
        scratch_shapes=[pltpu.VMEM((T, kw), bf16), pltpu.VMEM((T, HEAD_DIM), bf16),
                        pltpu.VMEM((m_rows, kw), bf16)],
        compiler_params=_params(("arbitrary", "arbitrary"), 52),
        name="flash_win" if windowed else "flash_slc",
    )(*args)


SLC_SLOTS = 128
SLC_TILE = 16


def _slc_sample_kernel(nb_past, pos_base, ids_ref, cnt_ref, pt_ref, q_ref, bias_ref, bnew_ref, knew_ref, vnew_ref,
                       pool_ref, oh_ref, o_ref, kbuf, vbuf, sem):
    step = pl.program_id(0) * N_KV + pl.program_id(1)
    total = pl.num_programs(0) * N_KV
    ts = q_ref.shape[0]
    n_pages = pt_ref.shape[1]
    rows_new = knew_ref.shape[2]
    blocks_per_page = PAGE_SIZE // L_SLC
    tk = SLC_TILE * L_SLC
    m_rows = GRP * ts
    sl = step % 2

    def schedule(stp, buf):
        bb = stp // N_KV
        gg = stp % N_KV

        def start_slot(s, _):
            bid = ids_ref[stp * SLC_SLOTS + s]
            rows_dst = pl.ds(pl.multiple_of(s * L_SLC, L_SLC), L_SLC)
            page = pt_ref[bb, jnp.minimum(bid // blocks_per_page, n_pages - 1)]
            rows = pl.ds(pl.multiple_of((bid % blocks_per_page) * L_SLC, L_SLC), L_SLC)
            pltpu.make_async_copy(pool_ref.at[page, rows, gg, :], kbuf.at[buf, rows_dst, :], sem.at[buf]).start()
            pltpu.make_async_copy(pool_ref.at[page, rows, N_KV + gg, :], vbuf.at[buf, rows_dst, :], sem.at[buf]).start()
            return 0

        lax.fori_loop(0, cnt_ref[stp], start_slot, 0)

    @pl.when(step == 0)
    def _():
        kbuf[...] = jnp.zeros(kbuf.shape, f32)
        vbuf[...] = jnp.zeros(vbuf.shape, f32)
        schedule(step, sl)

    @pl.when(step + 1 < total)
    def _():
        schedule(step + 1, 1 - sl)

    n_used = cnt_ref[step]

    def drain(c, _):
        pltpu.make_async_copy(pool_ref.at[0, pl.ds(0, L_SLC), 0, :], kbuf.at[sl, pl.ds(0, L_SLC), :], sem.at[sl]).wait()
        pltpu.make_async_copy(pool_ref.at[0, pl.ds(0, L_SLC), 0, :], vbuf.at[sl, pl.ds(0, L_SLC), :], sem.at[sl]).wait()
        return 0

    lax.fori_loop(0, n_used, drain, 0)

    qs = [(q_ref[:, r * HEAD_DIM:(r + 1) * HEAD_DIM] * (SCALE * LOG2E)).astype(bf16) for r in range(GRP)]
    nt = (((1,), (1,)), ((), ()))

    def with_bias(bias):
        return jnp.concatenate([jnp.concatenate([q, bias.astype(bf16)], axis=1) for q in qs], axis=0)

    def update(carry, s, v):
        m, l, acc = carry
        m_new = jnp.maximum(m, jnp.max(s, axis=-1, keepdims=True))
        alpha = jnp.exp2(m - m_new)
        p = jnp.exp2(s - m_new)
        l = alpha * l + jnp.sum(p, axis=-1, keepdims=True)
        acc = alpha * acc + jnp.dot(p.astype(bf16), v.astype(bf16), preferred_element_type=f32)
        return m_new, l, acc

    qa = with_bias(bias_ref[...])

    def tile(kt, carry):
        k0 = pl.multiple_of(kt * tk, tk)
        kb = jnp.concatenate([kbuf[sl, pl.ds(k0, tk), :].astype(bf16), oh_ref[pl.ds(k0, tk), :]], axis=1)
        s = lax.dot_general(qa, kb, nt, preferred_element_type=f32)
        return update(carry, s, vbuf[sl, pl.ds(k0, tk), :])

    carry = (jnp.full((m_rows, 1), M_INIT, f32), jnp.zeros((m_rows, 1), f32), jnp.zeros((m_rows, HEAD_DIM), f32))
    carry = lax.fori_loop(0, (n_used + SLC_TILE - 1) // SLC_TILE, tile, carry)

    kb = jnp.concatenate([knew_ref[0, 0].astype(bf16), oh_ref[pl.ds(0, rows_new), :]], axis=1)
    s = lax.dot_general(with_bias(bnew_ref[...]), kb, nt, preferred_element_type=f32)
    key = lax.broadcasted_iota(jnp.int32, (m_rows, rows_new), 1)
    t_row = lax.broadcasted_iota(jnp.int32, (m_rows, rows_new), 0) % ts
    s = jnp.where(nb_past * L_SLC + key <= pos_base + t_row, s, NEG)
    _, l, acc = update(carry, s, vnew_ref[0, 0])
    o = acc / l
    for r in range(GRP):
        o_ref[:, r * HEAD_DIM:(r + 1) * HEAD_DIM] = o[r * ts:(r + 1) * ts]


def _slc_sample(zs, bias, ids, cnt, bias_new, page_table, pool, newb, ts, pos_base):
    Bs = page_table.shape[0]
    n_pool = pool.shape[0]
    nb_past = page_table.shape[1] * (PAGE_SIZE // L_SLC)
    rows_new = newb.shape[2]
    assert ts * N_SEL == SLC_SLOTS and SLC_SLOTS % SLC_TILE == 0 and rows_new <= SLC_SLOTS * L_SLC
    assert pos_base >= nb_past * L_SLC
    rows = SLC_SLOTS * L_SLC
    onehot = (jnp.arange(rows)[:, None] // L_SLC == jnp.arange(SLC_SLOTS)[None, :]).astype(bf16)
    ids_flat = ids.reshape(Bs, ts, N_KV * SLC_SLOTS)[:, 0].reshape(-1)
    cnt_flat = cnt.reshape(Bs, ts, N_KV, SLC_SLOTS)[:, 0, :, 0].reshape(-1)
    grid_spec = pltpu.PrefetchScalarGridSpec(
        num_scalar_prefetch=3,
        grid=(Bs, N_KV),
        in_specs=[pl.BlockSpec((ts, GRP * HEAD_DIM), lambda b, g, *_: (b, g)),
                  pl.BlockSpec((ts, SLC_SLOTS), lambda b, g, *_: (b, g)),
                  pl.BlockSpec((ts, SLC_SLOTS), lambda b, g, *_: (b, g)),
                  pl.BlockSpec((1, 1, rows_new, HEAD_DIM), lambda b, g, *_: (b, g, 0, 0)),
                  pl.BlockSpec((1, 1, rows_new, HEAD_DIM), lambda b, g, *_: (b, N_KV + g, 0, 0)),
                  pl.BlockSpec(memory_space=pl.ANY),
                  pl.BlockSpec((rows, SLC_SLOTS), lambda b, g, *_: (0, 0))],
        out_specs=pl.BlockSpec((ts, GRP * HEAD_DIM), lambda b, g, *_: (b, g)),
        scratch_shapes=[pltpu.VMEM((2, rows, HEAD_DIM), f32), pltpu.VMEM((2, rows, HEAD_DIM), f32),
                        pltpu.SemaphoreType.DMA((2,))],
    )
    return pl.pallas_call(
        functools.partial(_slc_sample_kernel, nb_past, pos_base),
        grid_spec=grid_spec,
        out_shape=jax.ShapeDtypeStruct((Bs * ts, N_HEADS * HEAD_DIM), f32),
        compiler_params=_params(("arbitrary", "arbitrary"), 40),
        name="slc_sample",
    )(ids_flat, cnt_flat, page_table, zs, bias, bias_new, newb, newb,
      pool.reshape(n_pool, PAGE_SIZE, 2 * N_KV, HEAD_DIM), onehot)


def _win_sample_kernel(q_ref, new_ref, win_ref, o_ref, kall, sem):
    b = pl.program_id(0)
    ts = q_ref.shape[0]
    wb = win_ref.shape[1]
    rows = kall.shape[1]

    def copies():
        return [pltpu.make_async_copy(win_ref.at[b, :, eg, :], kall.at[eg, pl.ds(0, wb), :], sem.at[0])
                for eg in range(2 * N_KV)]

    for cp in copies():
        cp.start()
    for eg in range(2 * N_KV):
        kall[eg, wb:wb + ts, :] = new_ref[:, eg * HEAD_DIM:(eg + 1) * HEAD_DIM]
        kall[eg, wb + ts:rows, :] = jnp.zeros((rows - wb - ts, HEAD_DIM), f32)
    for cp in copies():
        cp.wait()

    m_rows = GRP * ts
    tq = lax.broadcasted_iota(jnp.int32, (m_rows, rows), 0) % ts
    ki = lax.broadcasted_iota(jnp.int32, (m_rows, rows), 1)
    d = tq + wb - ki
    ok = (d >= 0) & (d <= WINDOW) & (ki < wb + ts)
    for g in range(N_KV):
        qg = jnp.concatenate([q_ref[:, (g * GRP + r) * HEAD_DIM:(g * GRP + r + 1) * HEAD_DIM]
                              for r in range(GRP)], axis=0).astype(bf16)
        s = lax.dot_general(qg, kall[g].astype(bf16), (((1,), (1,)), ((), ())),
                            preferred_element_type=f32) * SCALE
        s = jnp.where(ok, s, -jnp.inf)
        mx = jnp.max(s, axis=-1, keepdims=True)
        mx = jnp.where(mx > -jnp.inf, mx, 0.0)
        ex = jnp.where(ok, jnp.exp(s - mx), 0.0)
        p = ex / jnp.maximum(jnp.sum(ex, axis=-1, keepdims=True), 1e-30)
        o = jnp.dot(p.astype(bf16), kall[N_KV + g].astype(bf16), preferred_element_type=f32)
        for r in range(GRP):
            o_ref[:, (g * GRP + r) * HEAD_DIM:(g * GRP + r + 1) * HEAD_DIM] = o[r * ts:(r + 1) * ts]


def _win_sample(zs, win_buf, ts):
    Bs, wb = win_buf.shape[0], win_buf.shape[1]
    rows = wb + HEAD_DIM
    ncol = 2 * KV_W
    return pl.pallas_call(
        _win_sample_kernel,
        grid=(Bs,),
        in_specs=[pl.BlockSpec((ts, N_HEADS * HEAD_DIM), lambda b: (b, COL_Q)),
                  pl.BlockSpec((ts, ncol), lambda b: (b, (COL_KV + 2 * ncol) // ncol)),
                  pl.BlockSpec(memory_space=pl.ANY)],
        out_specs=pl.BlockSpec((ts, N_HEADS * HEAD_DIM), lambda b: (b, 0)),
        out_shape=jax.ShapeDtypeStruct((Bs * ts, N_HEADS * HEAD_DIM), f32),
        scratch_shapes=[pltpu.VMEM((2 * N_KV, rows, HEAD_DIM), f32), pltpu.SemaphoreType.DMA((1,))],
        compiler_params=_params(("arbitrary",), 32),
        name="win_sample",
    )(zs, zs, win_buf.reshape(Bs, wb, 2 * N_KV, HEAD_DIM))


def _nsa_out_kernel(oc_ref, os_ref, ow_ref, gn_ref, bg_ref, w_ref, y_ref, u_s):
    bgs = jax.nn.sigmoid(bg_ref[...])
    gn = gn_ref[...]
    for h in range(N_HEADS):
        cols = slice(h * HEAD_DIM, (h + 1) * HEAD_DIM)
        o = (bgs[:, h:h + 1] * oc_ref[:, cols].astype(f32)
             + bgs[:, N_HEADS + h:N_HEADS + h + 1] * os_ref[:, cols].astype(f32)
             + bgs[:, 2 * N_HEADS + h:2 * N_HEADS + h + 1] * ow_ref[:, cols].astype(f32))
        gh = gn[:, cols]
        u_s[:, cols] = (o * (gh * jax.nn.sigmoid(gh))).astype(bf16)
    y_ref[...] = jnp.dot(u_s[...], w_ref[...], preferred_element_type=f32).astype(y_ref.dtype)


def _nsa_out(o_cmp, o_slc, o_win, z, w):
    T = z.shape[0]
    tm = min(T, 256)
    row = lambda i: (i, 0)
    return pl.pallas_call(
        _nsa_out_kernel,
        grid=(T // tm,),
        in_specs=[pl.BlockSpec((tm, D_MODEL), row)] * 3 + [
            pl.BlockSpec((tm, D_MODEL), lambda i: (i, COL_GN // D_MODEL)),
            pl.BlockSpec((tm, TN), lambda i: (i, COL_BG // TN)),
            _const_spec((D_MODEL, D_MODEL))],
        out_specs=pl.BlockSpec((tm, D_MODEL), row),
        out_shape=jax.ShapeDtypeStruct((T, D_MODEL), bf16),
        scratch_shapes=[pltpu.VMEM((tm, D_MODEL), bf16)],
        compiler_params=_params(("arbitrary",), 48),
        name="nsa_out",
    )(o_cmp, o_slc, o_win, z, z, w)


def _merge_kernel(yr_ref, yn_ref, m0_ref, m1_ref, x_ref, w_ref, o_ref):
    u = (jax.nn.sigmoid(m0_ref[...]) * yr_ref[...].astype(f32)
         + jax.nn.sigmoid(m1_ref[...]) * yn_ref[...].astype(f32))
    o_ref[...] = x_ref[...] + jnp.dot(u.astype(bf16), w_ref[...], preferred_element_type=f32)


def _merge(y_rnn, y_nsa, z, x, w):
    T = z.shape[0]
    tm = min(T, 256)
    row = lambda i: (i, 0)
    return pl.pallas_call(
        _merge_kernel,
        grid=(T // tm,),
        in_specs=[pl.BlockSpec((tm, D_MODEL), row)] * 2 + [
            pl.BlockSpec((tm, D_MODEL), lambda i: (i, COL_MG // D_MODEL)),
            pl.BlockSpec((tm, D_MODEL), lambda i: (i, COL_MG // D_MODEL + 1)),
            pl.BlockSpec((tm, D_MODEL), row), _const_spec((D_MODEL, D_MODEL))],
        out_specs=pl.BlockSpec((tm, D_MODEL), row),
        out_shape=jax.ShapeDtypeStruct((T, D_MODEL), f32),
        compiler_params=_params(("arbitrary",), 48),
        name="merge_out",
    )(y_rnn, y_nsa, z, z, x, w)


def _ple_kernel(x_ref, p_ref, wp_ref, wg_ref, gf_ref, o_ref):
    x = x_ref[...]
    emb = jnp.dot(p_ref[...].astype(bf16), wp_ref[...], preferred_element_type=f32)
    gate = jax.nn.sigmoid(jnp.dot(x.astype(bf16), wg_ref[...], preferred_element_type=f32))
    x = x + emb * gate
    ms = jnp.mean(x * x, axis=-1, keepdims=True)
    o_ref[...] = x * lax.rsqrt(ms + EPS) * gf_ref[...]


def _ple_norm(x1, p, w_ple, w_gate, g_final):
    T = x1.shape[0]
    tm = min(T, 256)
    row = lambda i: (i, 0)
    return pl.pallas_call(
        _ple_kernel,
        grid=(T // tm,),
        in_specs=[pl.BlockSpec((tm, D_MODEL), row), pl.BlockSpec((tm, PLE_DIM), row),
                  _const_spec((PLE_DIM, D_MODEL)), _const_spec((D_MODEL, D_MODEL)), _const_spec((1, D_MODEL))],
        out_specs=pl.BlockSpec((tm, D_MODEL), row),
        out_shape=jax.ShapeDtypeStruct((T, D_MODEL), f32),
        compiler_params=_params(("arbitrary",), 40),
        name="ple_norm",
    )(x1, p, w_ple, w_gate, g_final)


def kernel(x_prompt, x_sample, cache_cmp_kv, cache_slc_kv, state_win_kv, state_rnn_h, state_rnn_conv, page_table,
           p_prompt, p_sample, g_norm, w_in, w_conv, b_conv, w_rg, b_rg, lam, w_cmp1, w_cmp2, pe_cmp,
           w_rnn_proj, w_nsa_proj, w_out, w_ple, w_ple_gate, g_final):
    B, T, _ = x_prompt.shape
    Bs, Ts, _ = x_sample.shape
    past = page_table.shape[1] * PAGE_SIZE
    assert B == 1 and w_in.shape[0] == 1, "single prompt sequence, single layer"
    assert Ts == 8 and T % 1024 == 0 and T >= CONV_W - 1 and Ts >= CONV_W - 1 and Ts < L_CMP
    w_in_t = jnp.transpose(w_in[0])
    w_tail = _w_in_tail(w_in_t)
    rg_w = _rglru_weights(w_conv[0], b_conv[0], w_rg[0], b_rg[0], lam[0], w_rnn_proj[0])

    cos_p, sin_p = _rope_tables(jnp.arange(T))
    zp = _in_proj(x_prompt.reshape(T, D_MODEL), g_norm, w_in_t, w_tail, cos_p, sin_p)
    cos_s, sin_s = _rope_tables(jnp.tile(past + jnp.arange(Ts), Bs))
    zs = _in_proj(x_sample.reshape(Bs * Ts, D_MODEL), g_norm, w_in_t, w_tail, cos_s, sin_s)

    y_rnn_p, h_p = _rglru_prompt(zp, rg_w)
    stpad = jnp.pad(state_rnn_conv[0], ((0, 0), (8 - (CONV_W - 1), 0), (0, 0))).reshape(Bs * Ts, D_RNN)
    h0x = jnp.repeat(state_rnn_h[0], Ts, axis=0)
    y_rnn_s, hr_s = _rglru_sample(zs, stpad, h0x, rg_w)

    cw = _cmp_weights(w_cmp1[0], w_cmp2[0], pe_cmp[0])
    ckv_p = jnp.transpose(_compress_prompt(zp, cw), (3, 2, 0, 1, 4)).reshape(1, T // L_CMP, 2 * N_KV, HEAD_DIM)
    ckv_s = _compress_sample(page_table, cache_cmp_kv[0], cw)
    n_blk_p = -(-T // L_SLC)
    n_blk_s = -(-(past + Ts) // L_SLC)
    o_cmp_p, bias_p = _cmp_attn(zp, ckv_p, FLASH_T, 0, FLASH_T, n_blk_p, True)
    o_cmp_s, bias_s, ids_s, cnt_s, bnew_s = _cmp_attn(zs, ckv_s, Ts, past, 0, n_blk_s, False)
    o_slc_p = _flash_prompt(zp, 1, bias_p)
    o_win_p = _flash_prompt(zp, 2)

    ncol = 2 * KV_W
    nb_past = past // L_SLC
    new_rows = (n_blk_s - nb_past) * L_SLC
    newb = jnp.pad(jnp.transpose(zs[:, COL_KV + ncol:COL_KV + 2 * ncol].reshape(Bs, Ts, 2 * N_KV, HEAD_DIM),
                                 (0, 2, 1, 3)), ((0, 0), (0, 0), (0, new_rows - Ts), (0, 0)))
    o_slc_s = _slc_sample(zs, bias_s, ids_s, cnt_s, bnew_s, page_table, cache_slc_kv[0], newb, Ts, past)
    o_win_s = _win_sample(zs, state_win_kv[0], Ts)

    w_nsa_b, w_out_b = w_nsa_proj[0].astype(bf16), w_out[0].astype(bf16)
    w_ple_b, w_gate_b = w_ple[0].astype(bf16), w_ple_gate[0].astype(bf16)
    gf = g_final.reshape(1, D_MODEL)

    def tail(z, x, p, y_rnn, o_cmp, o_slc, o_win):
        y_nsa = _nsa_out(o_cmp, o_slc, o_win, z, w_nsa_b)
        x1 = _merge(y_rnn, y_nsa, z, x, w_out_b)
        return _ple_norm(x1, p, w_ple_b, w_gate_b, gf)

    y_p = tail(zp, x_prompt.reshape(T, D_MODEL), p_prompt.reshape(T, PLE_DIM), y_rnn_p, o_cmp_p, o_slc_p, o_win_p)
    y_s = tail(zs, x_sample.reshape(Bs * Ts, D_MODEL), p_sample.reshape(Bs * Ts, PLE_DIM), y_rnn_s, o_cmp_s, o_slc_s,
               o_win_s)

    kv_shape = (2, N_KV, HEAD_DIM)
    zs3 = zs.reshape(Bs, Ts, D_Z)
    keep_p = min(WINDOW, T)
    win_s = jnp.concatenate([state_win_kv[0], zs3[:, :, COL_KV + 2 * ncol:COL_KV + 3 * ncol].reshape(Bs, Ts, *kv_shape)],
                            axis=1)
    keep_s = min(WINDOW, past + Ts)
    hist = CONV_W - 1
    return (y_p.reshape(1, T, D_MODEL),
            y_s.reshape(Bs, Ts, D_MODEL),
            zp[:, COL_KV:COL_KV + ncol].reshape(1, 1, T, *kv_shape),
            zs3[:, :, COL_KV:COL_KV + ncol].reshape(1, Bs, Ts, *kv_shape),
            zp[:, COL_KV + ncol:COL_KV + 2 * ncol].reshape(1, 1, T, *kv_shape),
            zs3[:, :, COL_KV + ncol:COL_KV + 2 * ncol].reshape(1, Bs, Ts, *kv_shape),
            zp[T - keep_p:, COL_KV + 2 * ncol:COL_KV + 3 * ncol].reshape(1, 1, keep_p, *kv_shape),
            win_s[None, :, win_s.shape[1] - keep_s:],
            h_p[7:8].reshape(1, 1, D_RNN),
            hr_s.reshape(Bs, Ts, D_RNN)[None, :, Ts - 1],
            zp[T - hist:, COL_XR:COL_XR + D_RNN].reshape(1, 1, hist, D_RNN),
            zs3[None, :, Ts - hist:, COL_XR:COL_XR + D_RNN])
```

```python
import functools
import math

import jax
import jax.numpy as jnp
from jax import lax
from jax.experimental import pallas as pl
from jax.experimental.pallas import tpu as pltpu

f32 = jnp.float32
bf16 = jnp.bfloat16

D_MODEL = 2048
D_RNN = 2048
RNN_BLOCKS = 8
RNN_BLK = D_RNN // RNN_BLOCKS
CONV_W = 4
C_SCALE = 8.0
N_HEADS = 16
HEAD_DIM = 128
N_KV = 4
GRP = N_HEADS // N_KV
KV_W = N_KV * HEAD_DIM
L_CMP = 32
L_SLC = 64
PAGE_SIZE = 128
N_SEL = 16
N_LOCAL = 2
WINDOW = 512
FORCE = 1e4
SCALE = HEAD_DIM ** -0.5
ROPE_THETA = 10000.0
PLE_DIM = 256
EPS = 1e-6
NEG = -1e30

COL_Q = 0
COL_XR = 2048
COL_GR = 4096
COL_GN = 6144
COL_MG = 8192
COL_KV = 12288
COL_BG = 15360
D_Z = 15872
TN = 512
MIB = 1024 * 1024


def _params(sem, vmem_mib):
    return pltpu.CompilerParams(dimension_semantics=sem, vmem_limit_bytes=vmem_mib * MIB)


J_XR, J_GN, J_MG, J_KV, J_BG = COL_XR // TN, COL_GN // TN, COL_MG // TN, COL_KV // TN, COL_BG // TN
SRC_XR, SRC_Q, SRC_KV, SRC_GN, SRC_TAIL = 0, 4096 // TN, 6144 // TN, 9216 // TN, 11264


def _from_tail(j):
    return ((j >= J_MG) & (j < J_KV)) | (j >= J_BG)


def _w_src_block(j):
    return jnp.where(j < J_XR, j + SRC_Q,
                     jnp.where(j < J_GN, j - J_XR + SRC_XR,
                               jnp.where(j < J_MG, j - J_GN + SRC_GN,
                                         jnp.where(j < J_KV, J_MG - 1 - J_GN + SRC_GN,
                                                   jnp.where(j < J_BG, j - J_KV + SRC_KV, J_BG - 1 - J_KV + SRC_KV)))))


def _w_tail_block(j):
    n_mg = J_KV - J_MG
    return jnp.where(j < J_MG, 0, jnp.where(j < J_KV, j - J_MG, jnp.where(j < J_BG, n_mg - 1, n_mg)))


def _proj_kernel(x_ref, g_ref, w_ref, wt_ref, cos_ref, sin_ref, o_ref, hn_ref):
    j = pl.program_id(1)

    @pl.when(j == 0)
    def _():
        x = x_ref[...]
        ms = jnp.mean(x * x, axis=-1, keepdims=True)
        hn_ref[...] = (x * lax.rsqrt(ms + EPS) * g_ref[...]).astype(bf16)

    tail = _from_tail(j)
    nt = (((1,), (1,)), ((), ()))

    @pl.when(jnp.logical_not(tail))
    def _():
        o_ref[...] = lax.dot_general(hn_ref[...], w_ref[...].astype(bf16), nt, preferred_element_type=f32)

    @pl.when(tail)
    def _():
        o_ref[...] = lax.dot_general(hn_ref[...], wt_ref[...].astype(bf16), nt, preferred_element_type=f32)

    @pl.when((j < J_XR) | ((j >= J_KV) & (j < J_BG) & ((j - J_KV) % 2 == 0)))
    def _():
        c = cos_ref[...]
        s = sin_ref[...]
        for h in range(TN // HEAD_DIM):
            a = o_ref[:, h * HEAD_DIM:(h + 1) * HEAD_DIM]
            o_ref[:, h * HEAD_DIM:(h + 1) * HEAD_DIM] = a * c + pltpu.roll(a, HEAD_DIM // 2, 1) * s


def _in_proj(x, g_norm, w_in, w_tail, cos_t, sin_t):
    T = x.shape[0]
    tm = min(T, 1024)
    return pl.pallas_call(
        _proj_kernel,
        grid=(T // tm, D_Z // TN),
        in_specs=[
            pl.BlockSpec((tm, D_MODEL), lambda i, j: (i, 0)),
            pl.BlockSpec((1, D_MODEL), lambda i, j: (0, 0)),
            pl.BlockSpec((TN, D_MODEL), lambda i, j: (_w_src_block(j), 0)),
            pl.BlockSpec((TN, D_MODEL), lambda i, j: (_w_tail_block(j), 0)),
            pl.BlockSpec((tm, HEAD_DIM), lambda i, j: (i, 0)),
            pl.BlockSpec((tm, HEAD_DIM), lambda i, j: (i, 0)),
        ],
        out_specs=pl.BlockSpec((tm, TN), lambda i, j: (i, j)),
        out_shape=jax.ShapeDtypeStruct((T, D_Z), f32),
        scratch_shapes=[pltpu.VMEM((tm, D_MODEL), bf16)],
        compiler_params=_params(("arbitrary", "arbitrary"), 52),
        name="in_proj",
    )(x, g_norm, w_in, w_tail, cos_t, sin_t)


def _rope_tables(pos):
    half = HEAD_DIM // 2
    inv = ROPE_THETA ** (-jnp.arange(half, dtype=f32) / half)
    ang = pos.astype(f32)[:, None] * inv[None, :]
    cos = jnp.cos(ang)
    sin = jnp.sin(ang)
    return jnp.concatenate([cos, cos], axis=1), jnp.concatenate([-sin, sin], axis=1)


def _w_in_tail(wt):
    n_bg = 3 * N_HEADS
    pad = jnp.zeros((TN - n_bg, D_MODEL), wt.dtype)
    return jnp.concatenate([wt[SRC_TAIL + n_bg:], wt[SRC_TAIL:SRC_TAIL + n_bg], pad], axis=0)


def _expm1(x):
    u = jnp.exp(x)
    near = jnp.where(u == 1.0, x, (u - 1.0) * x / jnp.log(u))
    return jnp.where(x < -0.5, u - 1.0, near)


def _rglru_core(prompt, xr_ref, gr_ref, st_ref, h0_ref, wconv_ref, bconv_ref, wrg_ref, brg_ref, c_ref,
                wproj_ref, y_ref, hout_ref, a_s, b_s, tail_s, hc_s):
    tm = xr_ref.shape[0]
    groups = tm // 8
    if prompt:
        @pl.when(pl.program_id(0) == 0)
        def _():
            tail_s[...] = jnp.zeros_like(tail_s)
            hc_s[...] = jnp.zeros_like(hc_s)

    row = lax.broadcasted_iota(jnp.int32, (tm, RNN_BLK), 0)
    t8 = row % 8
    tseq = row if prompt else t8
    for n in range(RNN_BLOCKS):
        cols = slice(n * RNN_BLK, (n + 1) * RNN_BLK)
        xr = xr_ref[:, cols]
        xc = jnp.broadcast_to(bconv_ref[:, cols], (tm, RNN_BLK))
        for k in range(CONV_W):
            s = CONV_W - 1 - k
            if s == 0:
                term = xr
            else:
                if prompt:
                    hist = jnp.tile(pltpu.roll(tail_s[:, cols], s, 0), (groups, 1))
                else:
                    hist = pltpu.roll(st_ref[:, cols], tm - 8 + s, 0)
                term = jnp.where(tseq >= s, pltpu.roll(xr, s, 0), hist)
            xc = xc + term * wconv_ref[k:k + 1, cols]
        g = jnp.dot(xc.astype(bf16), wrg_ref[n], preferred_element_type=f32)
        r = jax.nn.sigmoid(g[:, :RNN_BLK] + brg_ref[0:1, cols])
        i = jax.nn.sigmoid(g[:, RNN_BLK:] + brg_ref[1:2, cols])
        log_a = c_ref[:, cols] * r
        a = jnp.exp(log_a)
        b = jnp.sqrt(-_expm1(2.0 * log_a)) * (i * xc)
        for s in (1, 2, 4):
            m = t8 >= s
            a_sh = pltpu.roll(a, s, 0)
            b_sh = pltpu.roll(b, s, 0)
            b = jnp.where(m, a * b_sh + b, b)
            a = jnp.where(m, a * a_sh, a)
        a_s[:, cols] = a
        b_s[:, cols] = b

    if prompt:
        tail_s[...] = xr_ref[tm - 8:tm, :]

        def body(c, h):
            r0 = pl.multiple_of(c * 8, 8)
            hg = a_s[pl.ds(r0, 8), :] * h + b_s[pl.ds(r0, 8), :]
            b_s[pl.ds(r0, 8), :] = hg
            return jnp.broadcast_to(hg[7:8, :], (8, D_RNN))

        h = lax.fori_loop(0, groups, body, hc_s[...])
        hc_s[...] = h
        hout_ref[...] = h
    else:
        b_s[...] = a_s[...] * h0_ref[...] + b_s[...]
        hout_ref[...] = b_s[...]

    gr = gr_ref[...]
    u = (b_s[...] * (gr * jax.nn.sigmoid(gr))).astype(bf16)
    y_ref[...] = jnp.dot(u, wproj_ref[...], preferred_element_type=f32).astype(y_ref.dtype)


def _rglru_prompt_kernel(xr_ref, gr_ref, wconv_ref, bconv_ref, wrg_ref, brg_ref, c_ref, wproj_ref,
                         y_ref, hout_ref, a_s, b_s, tail_s, hc_s):
    _rglru_core(True, xr_ref, gr_ref, None, None, wconv_ref, bconv_ref, wrg_ref, brg_ref, c_ref,
                wproj_ref, y_ref, hout_ref, a_s, b_s, tail_s, hc_s)


def _rglru_sample_kernel(xr_ref, gr_ref, st_ref, h0_ref, wconv_ref, bconv_ref, wrg_ref, brg_ref, c_ref,
                         wproj_ref, y_ref, hout_ref, a_s, b_s):
    _rglru_core(False, xr_ref, gr_ref, st_ref, h0_ref, wconv_ref, bconv_ref, wrg_ref, brg_ref, c_ref,
                wproj_ref, y_ref, hout_ref, a_s, b_s, None, None)


def _const_spec(shape):
    nd = len(shape)
    return pl.BlockSpec(shape, lambda *_: (0,) * nd)


def _rglru_weights(w_conv, b_conv, w_rg, b_rg, lam, w_rnn_proj):
    wrg = jnp.concatenate([w_rg[0], w_rg[1]], axis=-1).astype(bf16)
    c = (-C_SCALE * jax.nn.softplus(-lam.astype(f32))).reshape(1, D_RNN)
    return (w_conv, b_conv.reshape(1, D_RNN), wrg, b_rg, c, w_rnn_proj.astype(bf16))


def _rglru_wspecs():
    return [_const_spec((CONV_W, D_RNN)), _const_spec((1, D_RNN)), _const_spec((RNN_BLOCKS, RNN_BLK, 2 * RNN_BLK)),
            _const_spec((2, D_RNN)), _const_spec((1, D_RNN)), _const_spec((D_RNN, D_MODEL))]


def _rglru_prompt(z, weights):
    T = z.shape[0]
    tm = 256
    return pl.pallas_call(
        _rglru_prompt_kernel,
        grid=(T // tm,),
        in_specs=[pl.BlockSpec((tm, D_RNN), lambda i: (i, COL_XR // D_RNN)),
                  pl.BlockSpec((tm, D_RNN), lambda i: (i, COL_GR // D_RNN))] + _rglru_wspecs(),
        out_specs=[pl.BlockSpec((tm, D_MODEL), lambda i: (i, 0)), _const_spec((8, D_RNN))],
        out_shape=[jax.ShapeDtypeStruct((T, D_MODEL), bf16), jax.ShapeDtypeStruct((8, D_RNN), f32)],
        scratch_shapes=[pltpu.VMEM((tm, D_RNN), f32), pltpu.VMEM((tm, D_RNN), f32),
                        pltpu.VMEM((8, D_RNN), f32), pltpu.VMEM((8, D_RNN), f32)],
        compiler_params=_params(("arbitrary",), 56),
        name="rglru_prompt",
    )(z, z, *weights)


def _rglru_sample(z, stpad, h0x, weights):
    T = z.shape[0]
    return pl.pallas_call(
        _rglru_sample_kernel,
        grid=(1,),
        in_specs=[pl.BlockSpec((T, D_RNN), lambda i: (0, COL_XR // D_RNN)),
                  pl.BlockSpec((T, D_RNN), lambda i: (0, COL_GR // D_RNN)),
                  _const_spec((T, D_RNN)), _const_spec((T, D_RNN))] + _rglru_wspecs(),
        out_specs=[_const_spec((T, D_MODEL)), _const_spec((T, D_RNN))],
        out_shape=[jax.ShapeDtypeStruct((T, D_MODEL), bf16), jax.ShapeDtypeStruct((T, D_RNN), f32)],
        scratch_shapes=[pltpu.VMEM((T, D_RNN), f32), pltpu.VMEM((T, D_RNN), f32)],
        compiler_params=_params(("arbitrary",), 56),
        name="rglru_sample",
    )(z, z, stpad, h0x, *weights)


def _compress_mlp(load_rows, pe_ref, w1_ref, w2_ref, e):
    acc = None
    for lp in range(L_CMP // 2):
        parts = []
        for dl in range(2):
            l = 2 * lp + dl
            parts.append((load_rows(l) + pe_ref[e, l:l + 1, :]).astype(bf16))
        d = jnp.dot(jnp.concatenate(parts, axis=1), w1_ref[e, lp], preferred_element_type=f32)
        acc = d if acc is None else acc + d
    hid = acc * jax.nn.sigmoid(acc)
    return jnp.dot(hid.astype(bf16), w2_ref[e], preferred_element_type=f32)


def _compress_prompt_kernel(x_ref, pe_ref, w1_ref, w2_ref, o_ref):
    e = pl.program_id(0)
    nb = o_ref.shape[3]

    def load_rows(l):
        return jnp.concatenate([x_ref[pl.ds(l, nb, stride=2 * L_CMP), :],
                                x_ref[pl.ds(L_CMP + l, nb, stride=2 * L_CMP), :]], axis=0)

    res = _compress_mlp(load_rows, pe_ref, w1_ref, w2_ref, e)
    o_ref[0, 0, 0] = res[:nb]
    o_ref[0, 0, 1] = res[nb:]


def _cmp_weights(w_cmp1, w_cmp2, pe_cmp):
    w1 = w_cmp1.reshape(2, L_CMP // 2, 2 * HEAD_DIM, HEAD_DIM).astype(bf16)
    return pe_cmp, w1, w_cmp2.astype(bf16)


def _cmp_wspecs():
    return [_const_spec((2, L_CMP, HEAD_DIM)), _const_spec((2, L_CMP // 2, 2 * HEAD_DIM, HEAD_DIM)),
            _const_spec((2, HEAD_DIM, HEAD_DIM))]


def _compress_prompt(z, cw):
    T = z.shape[0]
    nb = T // (2 * L_CMP)
    col0 = COL_KV // HEAD_DIM
    return pl.pallas_call(
        _compress_prompt_kernel,
        grid=(2, N_KV),
        in_specs=[pl.BlockSpec((T, HEAD_DIM), lambda e, g: (0, col0 + e * N_KV + g))] + _cmp_wspecs(),
        out_specs=pl.BlockSpec((1, 1, 2, nb, HEAD_DIM), lambda e, g: (e, g, 0, 0, 0)),
        out_shape=jax.ShapeDtypeStruct((2, N_KV, 2, nb, HEAD_DIM), f32),
        compiler_params=_params(("arbitrary", "arbitrary"), 32),
        name="compress_prompt",
    )(z, *cw)


CMP_PAGES = 16


def _compress_sample_kernel(pt_ref, cache_ref, pe8_ref, w1_ref, w2_ref, o_ref, buf, sem):
    b = pl.program_id(0)
    c = pl.program_id(1)
    nchunk = pl.num_programs(1)
    step = b * nchunk + c
    total = pl.num_programs(0) * nchunk
    slot = step % 2
    nblk = CMP_PAGES * PAGE_SIZE // L_CMP
    rows = nblk * 2 * N_KV

    def copies(bb, cc, sl):
        return [pltpu.make_async_copy(cache_ref.at[pt_ref[bb, cc * CMP_PAGES + p]],
                                      buf.at[sl, pl.ds(p * PAGE_SIZE, PAGE_SIZE)], sem.at[sl])
                for p in range(CMP_PAGES)]

    @pl.when(step == 0)
    def _():
        for cp in copies(b, c, slot):
            cp.start()

    @pl.when(step + 1 < total)
    def _():
        nxt = step + 1
        for cp in copies(nxt // nchunk, nxt % nchunk, 1 - slot):
            cp.start()

    for cp in copies(b, c, slot):
        cp.wait()

    is_k = lax.broadcasted_iota(jnp.int32, (rows, HEAD_DIM), 0) % (2 * N_KV) < N_KV
    acc = None
    for lp in range(L_CMP // 2):
        parts = []
        for dl in range(2):
            l = 2 * lp + dl
            x = buf[slot, pl.ds(l, nblk, stride=L_CMP), :, :] + pe8_ref[l]
            parts.append(x.reshape(rows, HEAD_DIM).astype(bf16))
        d = jnp.dot(jnp.concatenate(parts, axis=1), w1_ref[lp], preferred_element_type=f32)
        acc = d if acc is None else acc + d
    hid = jnp.where(is_k, acc[:, :HEAD_DIM], acc[:, HEAD_DIM:])
    hid = hid * jax.nn.sigmoid(hid)
    res = jnp.dot(hid.astype(bf16), w2_ref[...], preferred_element_type=f32)
    res = jnp.where(is_k, res[:, :HEAD_DIM], res[:, HEAD_DIM:])
    o_ref[0] = res.reshape(nblk, 2 * N_KV, HEAD_DIM)


def _compress_sample(page_table, cache, cw):
    pe_cmp, w1, w2 = cw
    Bs, n_pages = page_table.shape
    n_pool = cache.shape[0]
    nchunk = n_pages // CMP_PAGES
    nblk = CMP_PAGES * PAGE_SIZE // L_CMP
    eg = 2 * N_KV
    pe8 = jnp.repeat(jnp.transpose(pe_cmp, (1, 0, 2)), N_KV, axis=1)
    w1cat = jnp.concatenate([w1[0], w1[1]], axis=-1)
    w2cat = jnp.concatenate([w2[0], w2[1]], axis=-1)
    grid_spec = pltpu.PrefetchScalarGridSpec(
        num_scalar_prefetch=1,
        grid=(Bs, nchunk),
        in_specs=[pl.BlockSpec(memory_space=pl.ANY),
                  pl.BlockSpec(pe8.shape, lambda b, c, pt: (0, 0, 0)),
                  pl.BlockSpec(w1cat.shape, lambda b, c, pt: (0, 0, 0)),
                  pl.BlockSpec(w2cat.shape, lambda b, c, pt: (0, 0))],
        out_specs=pl.BlockSpec((1, nblk, eg, HEAD_DIM), lambda b, c, pt: (b, c, 0, 0)),
        scratch_shapes=[pltpu.VMEM((2, CMP_PAGES * PAGE_SIZE, eg, HEAD_DIM), f32), pltpu.SemaphoreType.DMA((2,))],
    )
    return pl.pallas_call(
        _compress_sample_kernel,
        grid_spec=grid_spec,
        out_shape=jax.ShapeDtypeStruct((Bs, nchunk * nblk, eg, HEAD_DIM), f32),
        compiler_params=_params(("arbitrary", "arbitrary"), 40),
        name="compress_sample",
    )(page_table, cache.reshape(n_pool, PAGE_SIZE, eg, HEAD_DIM), pe8, w1cat, w2cat)


def _cmp_attn_kernel(pos_base, pos_step, n_blk, as_bias, per_seq, q_ref, ckv_ref, o_ref, sel_ref, *rest):
    if as_bias:
        ck_s, sem = rest
    else:
        ids_ref, cnt_ref, new_ref, ck_s, sem = rest
    i = pl.program_id(0)
    tq = q_ref.shape[0]
    nb = ckv_ref.shape[1]
    nc = 2 * nb

    def copies(seq, sl):
        return [pltpu.make_async_copy(ckv_ref.at[seq, :, par, eg, :], ck_s.at[sl, eg, pl.ds(par * nb, nb), :],
                                      sem.at[sl])
                for eg in range(2 * N_KV) for par in range(2)]

    if per_seq:
        slot = i % 2

        @pl.when(i == 0)
        def _():
            for cp in copies(i, slot):
                cp.start()

        @pl.when(i + 1 < pl.num_programs(0))
        def _():
            for cp in copies(i + 1, 1 - slot):
                cp.start()

        for cp in copies(i, slot):
            cp.wait()
    else:
        slot = 0

        @pl.when(i == 0)
        def _():
            for cp in copies(0, 0):
                cp.start()
            for cp in copies(0, 0):
                cp.wait()

    width = sel_ref.shape[-1] if as_bias else pl.cdiv(n_blk, HEAD_DIM) * HEAD_DIM
    m_rows = GRP * tq
    pos0 = pos_base + i * pos_step
    pos_col = pos0 + lax.broadcasted_iota(jnp.int32, (tq, 1), 0)
    pos_rows = jnp.concatenate([pos_col] * GRP, axis=0)
    col = lax.broadcasted_iota(jnp.int32, (m_rows, nc), 1)
    c_idx = 2 * (col % nb) + col // nb
    readable = ((c_idx + 1) * L_CMP - 1) <= pos_rows

    lane = lax.broadcasted_iota(jnp.int32, (tq, width), 1).astype(f32)
    jt = (pos_col // L_SLC).astype(f32)
    forced = (lane == 0.0) | ((lane >= jt - (N_LOCAL - 1)) & (lane <= jt))
    future = lane > jt
    exists = lane < float(n_blk)

    scores = []
    for g in range(N_KV):
        ck = ck_s[slot, g].astype(bf16)
        cv = ck_s[slot, N_KV + g].astype(bf16)
        qg = jnp.concatenate([q_ref[:, (g * GRP + r) * HEAD_DIM:(g * GRP + r + 1) * HEAD_DIM]
                              for r in range(GRP)], axis=0).astype(bf16)
        s = lax.dot_general(qg, ck, (((1,), (1,)), ((), ())), preferred_element_type=f32) * SCALE
        s = jnp.where(readable, s, -jnp.inf)
        mx = jnp.max(s, axis=-1, keepdims=True)
        mx = jnp.where(mx > -jnp.inf, mx, 0.0)
        ex = jnp.where(readable, jnp.exp(s - mx), 0.0)
        p = ex / jnp.maximum(jnp.sum(ex, axis=-1, keepdims=True), 1e-30)
        o = jnp.dot(p.astype(bf16), cv, preferred_element_type=f32)
        for r in range(GRP):
            o_ref[:, (g * GRP + r) * HEAD_DIM:(g * GRP + r + 1) * HEAD_DIM] = o[r * tq:(r + 1) * tq].astype(o_ref.dtype)

        pr = p[0:tq]
        for r in range(1, GRP):
            pr = pr + p[r * tq:(r + 1) * tq]
        ps = pr[:, :nb] + pr[:, nb:]
        if width > nb:
            ps = jnp.concatenate([ps, jnp.zeros((tq, width - nb), f32)], axis=1)
        scores.append(jnp.where(future, -jnp.inf, jnp.where(forced, FORCE, ps)))

    score = jnp.concatenate(scores, axis=0)
    lane = jnp.concatenate([lane] * N_KV, axis=0)
    exists = jnp.concatenate([exists] * N_KV, axis=0)
    removed = jnp.logical_not(exists)
    for k in range(N_SEL):
        se = jnp.where(removed, -jnp.inf, score)
        top = jnp.max(se, axis=-1, keepdims=True)
        cand = jnp.where((se == top) & jnp.logical_not(removed), lane, float(width))
        idx = jnp.min(cand, axis=-1, keepdims=True)
        removed = removed | (lane == idx)
    selected = removed & exists
    if as_bias:
        for g in range(N_KV):
            sel_ref[g] = jnp.where(selected[g * tq:(g + 1) * tq], 0.0, NEG).astype(bf16)
        return

    nb_past = pos_base // L_SLC
    assert nb_past + SLC_SLOTS <= width
    blk = lax.broadcasted_iota(jnp.int32, (tq, width), 1)
    cached = (blk < nb_past).astype(f32)
    half_id = (blk // 2).astype(bf16)
    par_id = (blk % 2).astype(bf16)
    before = (lax.broadcasted_iota(jnp.int32, (width, width), 0)
              < lax.broadcasted_iota(jnp.int32, (width, width), 1)).astype(bf16)
    slot_row = lax.broadcasted_iota(jnp.int32, (SLC_SLOTS, width), 0).astype(f32)
    nt = (((1,), (1,)), ((), ()))
    for g in range(N_KV):
        mem_all = selected[g * tq:(g + 1) * tq]
        new_ref[:, g * SLC_SLOTS:(g + 1) * SLC_SLOTS] = jnp.where(mem_all[:, nb_past:nb_past + SLC_SLOTS], 0.0, NEG)
        mem = mem_all.astype(f32) * cached
        uni = jnp.broadcast_to(jnp.max(mem, axis=0, keepdims=True), (tq, width))
        slot_of = jnp.dot(uni.astype(bf16), before, preferred_element_type=f32)
        count = jnp.sum(uni, axis=-1, keepdims=True)
        place = ((slot_row == slot_of[0:1, :]) & (uni[0:1, :] > 0.0)).astype(bf16)
        sel_slot = lax.dot_general(mem.astype(bf16), place, nt, preferred_element_type=f32)
        ids = (2.0 * lax.dot_general(half_id, place, nt, preferred_element_type=f32)
               + lax.dot_general(par_id, place, nt, preferred_element_type=f32))
        cols = slice(g * SLC_SLOTS, (g + 1) * SLC_SLOTS)
        sel_ref[:, cols] = jnp.where(sel_slot > 0.5, 0.0, NEG)
        ids_ref[:, cols] = ids.astype(jnp.int32)
        cnt_ref[:, cols] = jnp.broadcast_to(count, (tq, SLC_SLOTS)).astype(jnp.int32)


def _cmp_attn(z, ckv, tq, pos_base, pos_step, n_blk, as_bias):
    T = z.shape[0]
    n_seq, nc = ckv.shape[0], ckv.shape[1]
    nb = nc // 2
    per_seq = n_seq > 1
    assert n_seq == (T // tq if per_seq else 1)
    if as_bias:
        sel_specs = [pl.BlockSpec((N_KV, tq, n_blk), lambda i: (0, i, 0))]
        sel_shapes = [jax.ShapeDtypeStruct((N_KV, T, n_blk), bf16)]
    else:
        assert tq * N_SEL == SLC_SLOTS
        sel_specs = [pl.BlockSpec((tq, N_KV * SLC_SLOTS), lambda i: (i, 0))] * 4
        sel_shapes = [jax.ShapeDtypeStruct((T, N_KV * SLC_SLOTS), dt) for dt in (f32, jnp.int32, jnp.int32, f32)]
    kern = functools.partial(_cmp_attn_kernel, pos_base, pos_step, n_blk, as_bias, per_seq)
    return pl.pallas_call(
        kern,
        grid=(T // tq,),
        in_specs=[pl.BlockSpec((tq, N_HEADS * HEAD_DIM), lambda i: (i, COL_Q)),
                  pl.BlockSpec(memory_space=pl.ANY)],
        out_specs=[pl.BlockSpec((tq, N_HEADS * HEAD_DIM), lambda i: (i, 0))] + sel_specs,
        out_shape=[jax.ShapeDtypeStruct((T, N_HEADS * HEAD_DIM), bf16 if tq % 16 == 0 else f32)] + sel_shapes,
        scratch_shapes=[pltpu.VMEM((2 if per_seq else 1, 2 * N_KV, nc, HEAD_DIM), f32),
                        pltpu.SemaphoreType.DMA((2 if per_seq else 1,))],
        compiler_params=_params(("arbitrary",), 40),
        name="cmp_attn",
    )(z, ckv.reshape(n_seq, nb, 2, 2 * N_KV, HEAD_DIM))


FLASH_T = 256
SLC_TK = 1024
WIN_TK = WINDOW + FLASH_T
CAST_ROWS = 512
M_INIT = -3e38
LOG2E = 1.4426950408889634


def _flash_kernel(windowed, tk, q_ref, k_ref, v_ref, *rest):
    if windowed:
        o_ref, kb_s, vb_s, qa_s = rest
    else:
        bias_ref, oh_ref, o_ref, kb_s, vb_s, qa_s = rest
    tq = q_ref.shape[0]
    T = k_ref.shape[0]
    m_rows = GRP * tq
    qi = pl.program_id(1)

    @pl.when(qi == 0)
    def _():
        def cast(c, _):
            r0 = pl.multiple_of(c * CAST_ROWS, CAST_ROWS)
            kb_s[pl.ds(r0, CAST_ROWS), 0:HEAD_DIM] = k_ref[pl.ds(r0, CAST_ROWS), :].astype(bf16)
            if not windowed:
                kb_s[pl.ds(r0, CAST_ROWS), HEAD_DIM:2 * HEAD_DIM] = oh_ref[pl.ds(r0, CAST_ROWS), :]
            vb_s[pl.ds(r0, CAST_ROWS), :] = v_ref[pl.ds(r0, CAST_ROWS), :].astype(bf16)
            return 0
        lax.fori_loop(0, T // CAST_ROWS, cast, 0)

    for r in range(GRP):
        qa_s[r * tq:(r + 1) * tq, 0:HEAD_DIM] = (
            q_ref[:, r * HEAD_DIM:(r + 1) * HEAD_DIM] * (SCALE * LOG2E)).astype(bf16)
        if not windowed:
            qa_s[r * tq:(r + 1) * tq, HEAD_DIM:2 * HEAD_DIM] = bias_ref[0]
    def tile(k0, carry, masked):
        m, l, acc = carry
        s = lax.dot_general(qa_s[...], kb_s[pl.ds(k0, tk), :], (((1,), (1,)), ((), ())),
                            preferred_element_type=f32)
        if masked:
            t_loc = lax.broadcasted_iota(jnp.int32, (m_rows, 1), 0) % tq
            d = (qi * tq + t_loc - k0) - lax.broadcasted_iota(jnp.int32, (m_rows, tk), 1)
            ok = (d >= 0) & (d <= WINDOW) if windowed else d >= 0
            s = jnp.where(ok, s, NEG)
        m_new = jnp.maximum(m, jnp.max(s, axis=-1, keepdims=True))
        alpha = jnp.exp2(m - m_new)
        p = jnp.exp2(s - m_new)
        l = alpha * l + jnp.sum(p, axis=-1, keepdims=True)
        acc = alpha * acc + jnp.dot(p.astype(bf16), vb_s[pl.ds(k0, tk), :], preferred_element_type=f32)
        return m_new, l, acc

    carry = (jnp.full((m_rows, 1), M_INIT, f32), jnp.zeros((m_rows, 1), f32), jnp.zeros((m_rows, HEAD_DIM), f32))
    if windowed:
        carry = tile(pl.multiple_of(jnp.maximum(qi * tq - WINDOW, 0), tq), carry, True)
    else:
        n_full = (qi * tq) // tk
        carry = lax.fori_loop(0, n_full, lambda kt, c: tile(pl.multiple_of(kt * tk, tk), c, False), carry)
        carry = tile(pl.multiple_of(n_full * tk, tk), carry, True)
    _, l, acc = carry
    o = (acc / l).astype(o_ref.dtype)
    for r in range(GRP):
        o_ref[:, r * HEAD_DIM:(r + 1) * HEAD_DIM] = o[r * tq:(r + 1) * tq]


def _flash_prompt(z, branch, bias=None):
    T = z.shape[0]
    tq = FLASH_T
    windowed = bias is None
    tk = WIN_TK if windowed else SLC_TK
    m_rows = GRP * tq
    assert WINDOW % tq == 0 and T % SLC_TK == 0 and SLC_TK % tq == 0 and T >= WIN_TK
    kcol = COL_KV // HEAD_DIM + branch * 2 * N_KV
    in_specs = [pl.BlockSpec((tq, GRP * HEAD_DIM), lambda g, i: (i, g)),
                pl.BlockSpec((T, HEAD_DIM), lambda g, i: (0, kcol + g)),
                pl.BlockSpec((T, HEAD_DIM), lambda g, i: (0, kcol + N_KV + g))]
    args = [z, z, z]
    kw = HEAD_DIM
    if not windowed:
        n_blk = bias.shape[-1]
        onehot = (jnp.arange(T)[:, None] // L_SLC == jnp.arange(n_blk)[None, :]).astype(bf16)
        in_specs += [pl.BlockSpec((1, tq, n_blk), lambda g, i: (g, i, 0)), _const_spec((T, n_blk))]
        args += [bias, onehot]
        kw = HEAD_DIM + n_blk
    return pl.pallas_call(
        functools.partial(_flash_kernel, windowed, tk),
        grid=(N_KV, T // tq),
        in_specs=in_specs,
        out_specs=pl.BlockSpec((tq, GRP * HEAD_DIM), lambda g, i: (i, g)),
        out_shape=jax.ShapeDtypeStruct((T, N_HEADS * HEAD_DIM), bf16),
        scratch_shapes=[pltpu.VMEM((T, kw), bf16), pltpu.VMEM((T, HEAD_DIM), bf16),
                        pltpu.VMEM((m_rows, kw), bf16)],
        compiler_params=_params(("arbitrary", "arbitrary"), 52),
        name="flash_win" if windowed else "flash_slc",
    )(*args)


SLC_SLOTS = 128
SLC_TILE = 32


def _slc_sample_kernel(nb_past, pos_base, ids_ref, cnt_ref, pt_ref, q_ref, bias_ref, bnew_ref, knew_ref, vnew_ref,
                       pool_ref, oh_ref, o_ref, kbuf, vbuf, sem):
    step = pl.program_id(0) * N_KV + pl.program_id(1)
    total = pl.num_programs(0) * N_KV
    ts = q_ref.shape[0]
    n_pages = pt_ref.shape[1]
    rows_new = knew_ref.shape[2]
    blocks_per_page = PAGE_SIZE // L_SLC
    tk = SLC_TILE * L_SLC
    m_rows = GRP * ts
    sl = step % 2

    def schedule(stp, buf):
        bb = stp // N_KV
        gg = stp % N_KV

        def start_slot(s, _):
            bid = ids_ref[stp * SLC_SLOTS + s]
            rows_dst = pl.ds(pl.multiple_of(s * L_SLC, L_SLC), L_SLC)
            page = pt_ref[bb, jnp.minimum(bid // blocks_per_page, n_pages - 1)]
            rows = pl.ds(pl.multiple_of((bid % blocks_per_page) * L_SLC, L_SLC), L_SLC)
            pltpu.make_async_copy(pool_ref.at[page, rows, gg, :], kbuf.at[buf, rows_dst, :], sem.at[buf]).start()
            pltpu.make_async_copy(pool_ref.at[page, rows, N_KV + gg, :], vbuf.at[buf, rows_dst, :], sem.at[buf]).start()
            return 0

        lax.fori_loop(0, cnt_ref[stp], start_slot, 0)

    @pl.when(step == 0)
    def _():
        kbuf[...] = jnp.zeros(kbuf.shape, f32)
        vbuf[...] = jnp.zeros(vbuf.shape, f32)
        schedule(step, sl)

    @pl.when(step + 1 < total)
    def _():
        schedule(step + 1, 1 - sl)

    n_used = cnt_ref[step]

    def drain(c, _):
        pltpu.make_async_copy(pool_ref.at[0, pl.ds(0, L_SLC), 0, :], kbuf.at[sl, pl.ds(0, L_SLC), :], sem.at[sl]).wait()
        pltpu.make_async_copy(pool_ref.at[0, pl.ds(0, L_SLC), 0, :], vbuf.at[sl, pl.ds(0, L_SLC), :], sem.at[sl]).wait()
        return 0

    lax.fori_loop(0, n_used, drain, 0)

    qs = [(q_ref[:, r * HEAD_DIM:(r + 1) * HEAD_DIM] * (SCALE * LOG2E)).astype(bf16) for r in range(GRP)]
    nt = (((1,), (1,)), ((), ()))

    def with_bias(bias):
        return jnp.concatenate([jnp.concatenate([q, bias.astype(bf16)], axis=1) for q in qs], axis=0)

    def update(carry, s, v):
        m, l, acc = carry
        m_new = jnp.maximum(m, jnp.max(s, axis=-1, keepdims=True))
        alpha = jnp.exp2(m - m_new)
        p = jnp.exp2(s - m_new)
        l = alpha * l + jnp.sum(p, axis=-1, keepdims=True)
        acc = alpha * acc + jnp.dot(p.astype(bf16), v.astype(bf16), preferred_element_type=f32)
        return m_new, l, acc

    qa = with_bias(bias_ref[...])

    def tile(kt, carry):
        k0 = pl.multiple_of(kt * tk, tk)
        kb = jnp.concatenate([kbuf[sl, pl.ds(k0, tk), :].astype(bf16), oh_ref[pl.ds(k0, tk), :]], axis=1)
        s = lax.dot_general(qa, kb, nt, preferred_element_type=f32)
        return update(carry, s, vbuf[sl, pl.ds(k0, tk), :])

    carry = (jnp.full((m_rows, 1), M_INIT, f32), jnp.zeros((m_rows, 1), f32), jnp.zeros((m_rows, HEAD_DIM), f32))
    carry = lax.fori_loop(0, (n_used + SLC_TILE - 1) // SLC_TILE, tile, carry)

    kb = jnp.concatenate([knew_ref[0, 0].astype(bf16), oh_ref[pl.ds(0, rows_new), :]], axis=1)
    s = lax.dot_general(with_bias(bnew_ref[...]), kb, nt, preferred_element_type=f32)
    key = lax.broadcasted_iota(jnp.int32, (m_rows, rows_new), 1)
    t_row = lax.broadcasted_iota(jnp.int32, (m_rows, rows_new), 0) % ts
    s = jnp.where(nb_past * L_SLC + key <= pos_base + t_row, s, NEG)
    _, l, acc = update(carry, s, vnew_ref[0, 0])
    o = acc / l
    for r in range(GRP):
        o_ref[:, r * HEAD_DIM:(r + 1) * HEAD_DIM] = o[r * ts:(r + 1) * ts]


def _slc_sample(zs, bias, ids, cnt, bias_new, page_table, pool, newb, ts, pos_base):
    Bs = page_table.shape[0]
    n_pool = pool.shape[0]
    nb_past = page_table.shape[1] * (PAGE_SIZE // L_SLC)
    rows_new = newb.shape[2]
    assert ts * N_SEL == SLC_SLOTS and SLC_SLOTS % SLC_TILE == 0 and rows_new <= SLC_SLOTS * L_SLC
    assert pos_base >= nb_past * L_SLC
    rows = SLC_SLOTS * L_SLC
    onehot = (jnp.arange(rows)[:, None] // L_SLC == jnp.arange(SLC_SLOTS)[None, :]).astype(bf16)
    ids_flat = ids.reshape(Bs, ts, N_KV * SLC_SLOTS)[:, 0].reshape(-1)
    cnt_flat = cnt.reshape(Bs, ts, N_KV, SLC_SLOTS)[:, 0, :, 0].reshape(-1)
    grid_spec = pltpu.PrefetchScalarGridSpec(
        num_scalar_prefetch=3,
        grid=(Bs, N_KV),
        in_specs=[pl.BlockSpec((ts, GRP * HEAD_DIM), lambda b, g, *_: (b, g)),
                  pl.BlockSpec((ts, SLC_SLOTS), lambda b, g, *_: (b, g)),
                  pl.BlockSpec((ts, SLC_SLOTS), lambda b, g, *_: (b, g)),
                  pl.BlockSpec((1, 1, rows_new, HEAD_DIM), lambda b, g, *_: (b, g, 0, 0)),
                  pl.BlockSpec((1, 1, rows_new, HEAD_DIM), lambda b, g, *_: (b, N_KV + g, 0, 0)),
                  pl.BlockSpec(memory_space=pl.ANY),
                  pl.BlockSpec((rows, SLC_SLOTS), lambda b, g, *_: (0, 0))],
        out_specs=pl.BlockSpec((ts, GRP * HEAD_DIM), lambda b, g, *_: (b, g)),
        scratch_shapes=[pltpu.VMEM((2, rows, HEAD_DIM), f32), pltpu.VMEM((2, rows, HEAD_DIM), f32),
                        pltpu.SemaphoreType.DMA((2,))],
    )
    return pl.pallas_call(
        functools.partial(_slc_sample_kernel, nb_past, pos_base),
        grid_spec=grid_spec,
        out_shape=jax.ShapeDtypeStruct((Bs * ts, N_HEADS * HEAD_DIM), f32),
        compiler_params=_params(("arbitrary", "arbitrary"), 40),
        name="slc_sample",
    )(ids_flat, cnt_flat, page_table, zs, bias, bias_new, newb, newb,
      pool.reshape(n_pool, PAGE_SIZE, 2 * N_KV, HEAD_DIM), onehot)


def _win_sample_kernel(q_ref, new_ref, win_ref, o_ref, kall, sem):
    b = pl.program_id(0)
    ts = q_ref.shape[0]
    wb = win_ref.shape[1]
    rows = kall.shape[1]

    def copies():
        return [pltpu.make_async_copy(win_ref.at[b, :, eg, :], kall.at[eg, pl.ds(0, wb), :], sem.at[0])
                for eg in range(2 * N_KV)]

    for cp in copies():
        cp.start()
    for eg in range(2 * N_KV):
        kall[eg, wb:wb + ts, :] = new_ref[:, eg * HEAD_DIM:(eg + 1) * HEAD_DIM]
        kall[eg, wb + ts:rows, :] = jnp.zeros((rows - wb - ts, HEAD_DIM), f32)
    for cp in copies():
        cp.wait()

    m_rows = GRP * ts
    tq = lax.broadcasted_iota(jnp.int32, (m_rows, rows), 0) % ts
    ki = lax.broadcasted_iota(jnp.int32, (m_rows, rows), 1)
    d = tq + wb - ki
    ok = (d >= 0) & (d <= WINDOW) & (ki < wb + ts)
    for g in range(N_KV):
        qg = jnp.concatenate([q_ref[:, (g * GRP + r) * HEAD_DIM:(g * GRP + r + 1) * HEAD_DIM]
                              for r in range(GRP)], axis=0).astype(bf16)
        s = lax.dot_general(qg, kall[g].astype(bf16), (((1,), (1,)), ((), ())),
                            preferred_element_type=f32) * SCALE
        s = jnp.where(ok, s, -jnp.inf)
        mx = jnp.max(s, axis=-1, keepdims=True)
        mx = jnp.where(mx > -jnp.inf, mx, 0.0)
        ex = jnp.where(ok, jnp.exp(s - mx), 0.0)
        p = ex / jnp.maximum(jnp.sum(ex, axis=-1, keepdims=True), 1e-30)
        o = jnp.dot(p.astype(bf16), kall[N_KV + g].astype(bf16), preferred_element_type=f32)
        for r in range(GRP):
            o_ref[:, (g * GRP + r) * HEAD_DIM:(g * GRP + r + 1) * HEAD_DIM] = o[r * ts:(r + 1) * ts]


def _win_sample(zs, win_buf, ts):
    Bs, wb = win_buf.shape[0], win_buf.shape[1]
    rows = wb + HEAD_DIM
    ncol = 2 * KV_W
    return pl.pallas_call(
        _win_sample_kernel,
        grid=(Bs,),
        in_specs=[pl.BlockSpec((ts, N_HEADS * HEAD_DIM), lambda b: (b, COL_Q)),
                  pl.BlockSpec((ts, ncol), lambda b: (b, (COL_KV + 2 * ncol) // ncol)),
                  pl.BlockSpec(memory_space=pl.ANY)],
        out_specs=pl.BlockSpec((ts, N_HEADS * HEAD_DIM), lambda b: (b, 0)),
        out_shape=jax.ShapeDtypeStruct((Bs * ts, N_HEADS * HEAD_DIM), f32),
        scratch_shapes=[pltpu.VMEM((2 * N_KV, rows, HEAD_DIM), f32), pltpu.SemaphoreType.DMA((1,))],
        compiler_params=_params(("arbitrary",), 32),
        name="win_sample",
    )(zs, zs, win_buf.reshape(Bs, wb, 2 * N_KV, HEAD_DIM))


def _nsa_out_kernel(oc_ref, os_ref, ow_ref, gn_ref, bg_ref, w_ref, y_ref, u_s):
    bgs = jax.nn.sigmoid(bg_ref[...])
    gn = gn_ref[...]
    for h in range(N_HEADS):
        cols = slice(h * HEAD_DIM, (h + 1) * HEAD_DIM)
        o = (bgs[:, h:h + 1] * oc_ref[:, cols].astype(f32)
             + bgs[:, N_HEADS + h:N_HEADS + h + 1] * os_ref[:, cols].astype(f32)
             + bgs[:, 2 * N_HEADS + h:2 * N_HEADS + h + 1] * ow_ref[:, cols].astype(f32))
        gh = gn[:, cols]
        u_s[:, cols] = (o * (gh * jax.nn.sigmoid(gh))).astype(bf16)
    y_ref[...] = jnp.dot(u_s[...], w_ref[...], preferred_element_type=f32).astype(y_ref.dtype)


def _nsa_out(o_cmp, o_slc, o_win, z, w):
    T = z.shape[0]
    tm = min(T, 256)
    row = lambda i: (i, 0)
    return pl.pallas_call(
        _nsa_out_kernel,
        grid=(T // tm,),
        in_specs=[pl.BlockSpec((tm, D_MODEL), row)] * 3 + [
            pl.BlockSpec((tm, D_MODEL), lambda i: (i, COL_GN // D_MODEL)),
            pl.BlockSpec((tm, TN), lambda i: (i, COL_BG // TN)),
            _const_spec((D_MODEL, D_MODEL))],
        out_specs=pl.BlockSpec((tm, D_MODEL), row),
        out_shape=jax.ShapeDtypeStruct((T, D_MODEL), bf16),
        scratch_shapes=[pltpu.VMEM((tm, D_MODEL), bf16)],
        compiler_params=_params(("arbitrary",), 48),
        name="nsa_out",
    )(o_cmp, o_slc, o_win, z, z, w)


def _merge_kernel(yr_ref, yn_ref, m0_ref, m1_ref, x_ref, w_ref, o_ref):
    u = (jax.nn.sigmoid(m0_ref[...]) * yr_ref[...].astype(f32)
         + jax.nn.sigmoid(m1_ref[...]) * yn_ref[...].astype(f32))
    o_ref[...] = x_ref[...] + jnp.dot(u.astype(bf16), w_ref[...], preferred_element_type=f32)


def _merge(y_rnn, y_nsa, z, x, w):
    T = z.shape[0]
    tm = min(T, 256)
    row = lambda i: (i, 0)
    return pl.pallas_call(
        _merge_kernel,
        grid=(T // tm,),
        in_specs=[pl.BlockSpec((tm, D_MODEL), row)] * 2 + [
            pl.BlockSpec((tm, D_MODEL), lambda i: (i, COL_MG // D_MODEL)),
            pl.BlockSpec((tm, D_MODEL), lambda i: (i, COL_MG // D_MODEL + 1)),
            pl.BlockSpec((tm, D_MODEL), row), _const_spec((D_MODEL, D_MODEL))],
        out_specs=pl.BlockSpec((tm, D_MODEL), row),
        out_shape=jax.ShapeDtypeStruct((T, D_MODEL), f32),
        compiler_params=_params(("arbitrary",), 48),
        name="merge_out",
    )(y_rnn, y_nsa, z, z, x, w)


def _ple_kernel(x_ref, p_ref, wp_ref, wg_ref, gf_ref, o_ref):
    x = x_ref[...]
    emb = jnp.dot(p_ref[...].astype(bf16), wp_ref[...], preferred_element_type=f32)
    gate = jax.nn.sigmoid(jnp.dot(x.astype(bf16), wg_ref[...], preferred_element_type=f32))
    x = x + emb * gate
    ms = jnp.mean(x * x, axis=-1, keepdims=True)
    o_ref[...] = x * lax.rsqrt(ms + EPS) * gf_ref[...]


def _ple_norm(x1, p, w_ple, w_gate, g_final):
    T = x1.shape[0]
    tm = min(T, 256)
    row = lambda i: (i, 0)
    return pl.pallas_call(
        _ple_kernel,
        grid=(T // tm,),
        in_specs=[pl.BlockSpec((tm, D_MODEL), row), pl.BlockSpec((tm, PLE_DIM), row),
                  _const_spec((PLE_DIM, D_MODEL)), _const_spec((D_MODEL, D_MODEL)), _const_spec((1, D_MODEL))],
        out_specs=pl.BlockSpec((tm, D_MODEL), row),
        out_shape=jax.ShapeDtypeStruct((T, D_MODEL), f32),
        compiler_params=_params(("arbitrary",), 40),
        name="ple_norm",
    )(x1, p, w_ple, w_gate, g_final)


def kernel(x_prompt, x_sample, cache_cmp_kv, cache_slc_kv, state_win_kv, state_rnn_h, state_rnn_conv, page_table,
           p_prompt, p_sample, g_norm, w_in, w_conv, b_conv, w_rg, b_rg, lam, w_cmp1, w_cmp2, pe_cmp,
           w_rnn_proj, w_nsa_proj, w_out, w_ple, w_ple_gate, g_final):
    B, T, _ = x_prompt.shape
    Bs, Ts, _ = x_sample.shape
    past = page_table.shape[1] * PAGE_SIZE
    assert B == 1 and w_in.shape[0] == 1, "single prompt sequence, single layer"
    assert Ts == 8 and T % 1024 == 0 and T >= CONV_W - 1 and Ts >= CONV_W - 1 and Ts < L_CMP
    w_in_t = jnp.transpose(w_in[0])
    w_tail = _w_in_tail(w_in_t)
    rg_w = _rglru_weights(w_conv[0], b_conv[0], w_rg[0], b_rg[0], lam[0], w_rnn_proj[0])

    cos_p, sin_p = _rope_tables(jnp.arange(T))
    zp = _in_proj(x_prompt.reshape(T, D_MODEL), g_norm, w_in_t, w_tail, cos_p, sin_p)
    cos_s, sin_s = _rope_tables(jnp.tile(past + jnp.arange(Ts), Bs))
    zs = _in_proj(x_sample.reshape(Bs * Ts, D_MODEL), g_norm, w_in_t, w_tail, cos_s, sin_s)

    y_rnn_p, h_p = _rglru_prompt(zp, rg_w)
    stpad = jnp.pad(state_rnn_conv[0], ((0, 0), (8 - (CONV_W - 1), 0), (0, 0))).reshape(Bs * Ts, D_RNN)
    h0x = jnp.repeat(state_rnn_h[0], Ts, axis=0)
    y_rnn_s, hr_s = _rglru_sample(zs, stpad, h0x, rg_w)

    cw = _cmp_weights(w_cmp1[0], w_cmp2[0], pe_cmp[0])
    ckv_p = jnp.transpose(_compress_prompt(zp, cw), (3, 2, 0, 1, 4)).reshape(1, T // L_CMP, 2 * N_KV, HEAD_DIM)
    ckv_s = _compress_sample(page_table, cache_cmp_kv[0], cw)
    n_blk_p = -(-T // L_SLC)
    n_blk_s = -(-(past + Ts) // L_SLC)
    o_cmp_p, bias_p = _cmp_attn(zp, ckv_p, FLASH_T, 0, FLASH_T, n_blk_p, True)
    o_cmp_s, bias_s, ids_s, cnt_s, bnew_s = _cmp_attn(zs, ckv_s, Ts, past, 0, n_blk_s, False)
    o_slc_p = _flash_prompt(zp, 1, bias_p)
    o_win_p = _flash_prompt(zp, 2)

    ncol = 2 * KV_W
    nb_past = past // L_SLC
    new_rows = (n_blk_s - nb_past) * L_SLC
    newb = jnp.pad(jnp.transpose(zs[:, COL_KV + ncol:COL_KV + 2 * ncol].reshape(Bs, Ts, 2 * N_KV, HEAD_DIM),
                                 (0, 2, 1, 3)), ((0, 0), (0, 0), (0, new_rows - Ts), (0, 0)))
    o_slc_s = _slc_sample(zs, bias_s, ids_s, cnt_s, bnew_s, page_table, cache_slc_kv[0], newb, Ts, past)
    o_win_s = _win_sample(zs, state_win_kv[0], Ts)

    w_nsa_b, w_out_b = w_nsa_proj[0].astype(bf16), w_out[0].astype(bf16)
    w_ple_b, w_gate_b = w_ple[0].astype(bf16), w_ple_gate[0].astype(bf16)
    gf = g_final.reshape(1, D_MODEL)

    def tail(z, x, p, y_rnn, o_cmp, o_slc, o_win):
        y_nsa = _nsa_out(o_cmp, o_slc, o_win, z, w_nsa_b)
        x1 = _merge(y_rnn, y_nsa, z, x, w_out_b)
        return _ple_norm(x1, p, w_ple_b, w_gate_b, gf)

    y_p = tail(zp, x_prompt.reshape(T, D_MODEL), p_prompt.reshape(T, PLE_DIM), y_rnn_p, o_cmp_p, o_slc_p, o_win_p)
    y_s = tail(zs, x_sample.reshape(Bs * Ts, D_MODEL), p_sample.reshape(Bs * Ts, PLE_DIM), y_rnn_s, o_cmp_s, o_slc_s,
               o_win_s)

    kv_shape = (2, N_KV, HEAD_DIM)
    zs3 = zs.reshape(Bs, Ts, D_Z)
    keep_p = min(WINDOW, T)
    win_s = jnp.concatenate([state_win_kv[0], zs3[:, :, COL_KV + 2 * ncol:COL_KV + 3 * ncol].reshape(Bs, Ts, *kv_shape)],
                            axis=1)
    keep_s = min(WINDOW, past + Ts)
    hist = CONV_W - 1
    return (y_p.reshape(1, T, D_MODEL),
            y_s.reshape(Bs, Ts, D_MODEL),
            zp[:, COL_KV:COL_KV + ncol].reshape(1, 1, T, *kv_shape),
            zs3[:, :, COL_KV:COL_KV + ncol].reshape(1, Bs, Ts, *kv_shape),
            zp[:, COL_KV + ncol:COL_KV + 2 * ncol].reshape(1, 1, T, *kv_shape),
            zs3[:, :, COL_KV + ncol:COL_KV + 2 * ncol].reshape(1, Bs, Ts, *kv_shape),
            zp[T - keep_p:, COL_KV + 2 * ncol:COL_KV + 3 * ncol].reshape(1, 1, keep_p, *kv_shape),
            win_s[None, :, win_s.shape[1] - keep_s:],
            h_p[7:8].reshape(1, 1, D_RNN),
            hr_s.reshape(Bs, Ts, D_RNN)[None, :, Ts - 1],
            zp[T - hist:, COL_XR:COL_XR + D_RNN].reshape(1, 1, hist, D_RNN),
            zs3[None, :, Ts - hist:, COL_XR:COL_XR + D_RNN])
```
